```python
import math
import jax, jax.numpy as jnp
from jax import lax
import numpy as np

D_MODEL = 4096
BATCH = 4
SEQ = 2048
DEPTH = 2
DEC_BATCH = 8
DEC_SEQ = 1
PAST_LEN = 16384
PAGE_SIZE = 128

N_META = 16
N_HEADS = 16
HEAD_DIM = 128
D_ATTN = N_HEADS * HEAD_DIM
D_SSM = D_MODEL // 2
SSM_GROUP = 16
SSM_GROUPS = D_SSM // SSM_GROUP
SSM_STATE = 64
D_LRU = D_MODEL // 2
LRU_HEADS = 16
LRU_HEAD_DIM = D_LRU // LRU_HEADS
CONV_W = 4
RG_C = 8.0
D_FF = -(-8 * D_MODEL // (3 * 256)) * 256
Q_BLOCK = 128
N_BRANCH = 3
N_IN = 3 * D_ATTN + D_SSM + 2 * D_LRU + N_BRANCH * D_MODEL
SB_BIAS_INIT = -6.0
EPS = 1e-6

kernel_name = "hybrid_stickbreak_s5_rglru_decode_step"


def rms_norm(x, g):
    xf = x.astype(jnp.float32)
    y = xf * lax.rsqrt(jnp.mean(xf * xf, axis=-1, keepdims=True) + EPS)
    return (y * g.astype(jnp.float32)).astype(x.dtype)


def stick_breaking_attention(q, k, v, bias, q_start, bounds):
    scale = HEAD_DIM ** -0.5
    bf = bias.astype(jnp.float32)[None, :, None, None]
    outs = []
    for lo, hi in zip(bounds[:-1], bounds[1:]):
        n_keys = max(q_start + hi - 1, 1)
        qb = q[:, lo:hi].astype(jnp.float32)
        kb = k[:, :n_keys].astype(jnp.float32)
        vb = v[:, :n_keys].astype(jnp.float32)
        z = jnp.einsum('bqhd,bkhd->bhqk', qb, kb) * scale + bf
        q_pos = q_start + lo + jnp.arange(hi - lo)
        k_pos = jnp.arange(n_keys)
        causal = k_pos[None, :] < q_pos[:, None]
        log_beta = jax.nn.log_sigmoid(z)
        log_keep = jnp.where(causal, jax.nn.log_sigmoid(-z), 0.0)
        later = lax.cumsum(log_keep, axis=3, reverse=True) - log_keep
        w = jnp.where(causal, jnp.exp(log_beta + later), 0.0)
        outs.append(jnp.einsum('bhqk,bkhd->bqhd', w, vb))
    return jnp.concatenate(outs, axis=1).astype(q.dtype)


def linear_scan_real(a, b, h0):
    def combine(e1, e2):
        a1, b1 = e1
        a2, b2 = e2
        return a1 * a2, a2 * b1 + b2
    a_cum, b_cum = lax.associative_scan(combine, (a, b), axis=1)
    return a_cum * h0[:, None] + b_cum


def linear_scan_complex(a_re, a_im, b_re, b_im, h0_re, h0_im):
    def combine(e1, e2):
        a1r, a1i, b1r, b1i = e1
        a2r, a2i, b2r, b2i = e2
        return (a1r * a2r - a1i * a2i, a1r * a2i + a1i * a2r,
                a2r * b1r - a2i * b1i + b2r, a2r * b1i + a2i * b1r + b2i)
    pr, pim, sr, si = lax.associative_scan(combine, (a_re, a_im, b_re, b_im), axis=1)
    h0r = h0_re[:, None]
    h0i = h0_im[:, None]
    return pr * h0r - pim * h0i + sr, pr * h0i + pim * h0r + si


def s5_branch(u, h0_re, h0_im, p):
    bsz, t_len, _ = u.shape
    uf = u.astype(jnp.float32)
    ug = uf.reshape(bsz, t_len, SSM_GROUPS, SSM_GROUP)
    lam_re = p['ssm_a_re'].astype(jnp.float32)
    lam_im = p['ssm_a_im'].astype(jnp.float32)
    dt = jnp.exp(p['ssm_log_step'].astype(jnp.float32))[:, None]
    mag = jnp.exp(lam_re * dt)
    ang = lam_im * dt
    ab_re = mag * jnp.cos(ang)
    ab_im = mag * jnp.sin(ang)
    den = lam_re * lam_re + lam_im * lam_im
    f_re = ((ab_re - 1.0) * lam_re + ab_im * lam_im) / den
    f_im = (ab_im * lam_re - (ab_re - 1.0) * lam_im) / den
    b_re = p['ssm_b_re'].astype(jnp.float32)
    b_im = p['ssm_b_im'].astype(jnp.float32)
    bb_re = f_re[..., None] * b_re - f_im[..., None] * b_im
    bb_im = f_re[..., None] * b_im + f_im[..., None] * b_re
    bu_re = jnp.einsum('btgc,gpc->btgp', ug, bb_re)
    bu_im = jnp.einsum('btgc,gpc->btgp', ug, bb_im)
    a_re = jnp.broadcast_to(ab_re, bu_re.shape)
    a_im = jnp.broadcast_to(ab_im, bu_im.shape)
    h_re, h_im = linear_scan_complex(a_re, a_im, bu_re, bu_im,
                                     h0_re.astype(jnp.float32), h0_im.astype(jnp.float32))
    c_re = p['ssm_c_re'].astype(jnp.float32)
    c_im = p['ssm_c_im'].astype(jnp.float32)
    y = jnp.einsum('btgp,gcp->btgc', h_re, c_re) - jnp.einsum('btgp,gcp->btgc', h_im, c_im)
    y = y.reshape(bsz, t_len, D_SSM) + p['ssm_d'].astype(jnp.float32) * uf
    zg = jax.nn.gelu(y)
    zg = zg * jax.nn.sigmoid(zg @ p['ssm_w_glu'].astype(jnp.float32) + p['ssm_b_glu'].astype(jnp.float32))
    return zg.astype(u.dtype), h_re[:, -1].astype(u.dtype), h_im[:, -1].astype(u.dtype)


def rglru_branch(xb, gb, conv_buf, h0, p):
    bsz, t_len, _ = xb.shape
    xpad = jnp.concatenate([conv_buf.astype(xb.dtype), xb], axis=1)
    w = p['lru_conv_w']
    xc = xpad[:, 0:t_len] * w[0]
    for j in range(1, CONV_W):
        xc = xc + xpad[:, j:j + t_len] * w[j]
    xf = (xc + p['lru_conv_b']).astype(jnp.float32)
    xh = xf.reshape(bsz, t_len, LRU_HEADS, LRU_HEAD_DIM)
    r = jax.nn.sigmoid(jnp.einsum('bthi,hij->bthj', xh, p['lru_w_a'].astype(jnp.float32)).reshape(bsz, t_len, D_LRU)
                       + p['lru_b_a'].astype(jnp.float32))
    ig = jax.nn.sigmoid(jnp.einsum('bthi,hij->bthj', xh, p['lru_w_x'].astype(jnp.float32)).reshape(bsz, t_len, D_LRU)
                        + p['lru_b_x'].astype(jnp.float32))
    log_a = -RG_C * r * jax.nn.softplus(-p['lru_lambda'].astype(jnp.float32))
    a = jnp.exp(log_a)
    mult = jnp.sqrt(-jnp.expm1(2.0 * log_a))
    h = linear_scan_real(a, mult * ig * xf, h0.astype(jnp.float32))
    out = h * jax.nn.gelu(gb.astype(jnp.float32))
    return out.astype(xb.dtype), h[:, -1].astype(xb.dtype), xpad[:, -(CONV_W - 1):]


def decoder_layer(x, p, past_k, past_v, ssm_re, ssm_im, lru_h, conv_buf, q_start, bounds):
    bsz, t_len, _ = x.shape
    xn = rms_norm(x, p['norm_pre_mix'])
    proj = xn @ p['w_in']
    cuts = [D_ATTN, 2 * D_ATTN, 3 * D_ATTN, 3 * D_ATTN + D_SSM,
            3 * D_ATTN + D_SSM + D_LRU, 3 * D_ATTN + D_SSM + 2 * D_LRU]
    q, k, v, u_ssm, x_lru, g_lru, gate_logits = jnp.split(proj, cuts, axis=-1)
    q = q.reshape(bsz, t_len, N_HEADS, HEAD_DIM)
    k = k.reshape(bsz, t_len, N_HEADS, HEAD_DIM)
    v = v.reshape(bsz, t_len, N_HEADS, HEAD_DIM)
    if past_k is None:
        k_all, v_all = k, v
    else:
        k_all = jnp.concatenate([past_k.astype(k.dtype), k], axis=1)
        v_all = jnp.concatenate([past_v.astype(v.dtype), v], axis=1)
    o_attn = stick_breaking_attention(q, k_all, v_all, p['attn_bias'], q_start, bounds).reshape(bsz, t_len, D_ATTN)
    o_ssm, ssm_re_new, ssm_im_new = s5_branch(u_ssm, ssm_re, ssm_im, p)
    o_lru, lru_h_new, conv_new = rglru_branch(x_lru, g_lru, conv_buf, lru_h, p)
    gates = jax.nn.sigmoid(gate_logits.astype(jnp.float32)).reshape(bsz, t_len, N_BRANCH, D_MODEL)
    merged = (gates[:, :, 0] * (o_attn @ p['w_attn_out']).astype(jnp.float32)
              + gates[:, :, 1] * (o_ssm @ p['w_ssm_out']).astype(jnp.float32)
              + gates[:, :, 2] * (o_lru @ p['w_lru_out']).astype(jnp.float32))
    mix = merged.astype(x.dtype) @ p['w_out']
    x = x + rms_norm(mix, p['norm_post_mix'])
    hn = rms_norm(x, p['norm_pre_ffn'])
    ff = (jax.nn.silu(hn @ p['w_ffn_gate']) * (hn @ p['w_ffn_up'])) @ p['w_ffn_down']
    x = x + rms_norm(ff, p['norm_post_ffn'])
    return x, k, v, ssm_re_new, ssm_im_new, lru_h_new, conv_new


def setup_inputs(seed: int = 0) -> dict:
    key = jax.random.key(seed)
    ks = jax.random.split(key, 40)
    nrm = jax.random.normal
    f32 = jnp.float32
    n_pages = PAST_LEN // PAGE_SIZE
    n_pool = (DEC_BATCH * n_pages * 5) // 4
    x_prompt = nrm(ks[0], (BATCH, SEQ, D_MODEL), f32)
    x_sample = nrm(ks[1], (DEC_BATCH, DEC_SEQ, D_MODEL), f32)
    cache_k = nrm(ks[2], (DEPTH, n_pool, PAGE_SIZE, N_HEADS, HEAD_DIM), f32)
    cache_v = nrm(ks[3], (DEPTH, n_pool, PAGE_SIZE, N_HEADS, HEAD_DIM), f32)
    page_table = jax.random.permutation(ks[4], n_pool)[:DEC_BATCH * n_pages].reshape(DEC_BATCH, n_pages).astype(jnp.int32)
    state_ssm_re = 0.1 * nrm(ks[5], (DEPTH, DEC_BATCH, SSM_GROUPS, SSM_STATE), f32)
    state_ssm_im = 0.1 * nrm(ks[6], (DEPTH, DEC_BATCH, SSM_GROUPS, SSM_STATE), f32)
    state_lru_h = 0.5 * nrm(ks[7], (DEPTH, DEC_BATCH, D_LRU), f32)
    state_conv = nrm(ks[8], (DEPTH, DEC_BATCH, CONV_W - 1, D_LRU), f32)
    meta_tokens = nrm(ks[9], (N_META, D_MODEL), f32)
    w_in = nrm(ks[10], (DEPTH, D_MODEL, N_IN), f32) * D_MODEL ** -0.5
    attn_bias = SB_BIAS_INIT + 0.1 * nrm(ks[39], (DEPTH, N_HEADS), f32)
    ssm_a_re = -0.5 + 0.01 * nrm(ks[11], (DEPTH, SSM_GROUPS, SSM_STATE), f32)
    ssm_a_im = (jnp.pi * jnp.arange(SSM_STATE, dtype=f32))[None, None, :] + 0.01 * nrm(ks[12], (DEPTH, SSM_GROUPS, SSM_STATE), f32)
    ssm_log_step = math.log(1e-3) + jax.random.uniform(ks[13], (DEPTH, SSM_GROUPS), f32) * (math.log(1e-1) - math.log(1e-3))
    ssm_b_re = nrm(ks[14], (DEPTH, SSM_GROUPS, SSM_STATE, SSM_GROUP), f32) * (2 * SSM_GROUP) ** -0.5
    ssm_b_im = nrm(ks[15], (DEPTH, SSM_GROUPS, SSM_STATE, SSM_GROUP), f32) * (2 * SSM_GROUP) ** -0.5
    ssm_c_re = nrm(ks[16], (DEPTH, SSM_GROUPS, SSM_GROUP, SSM_STATE), f32) * (2 * SSM_STATE) ** -0.5
    ssm_c_im = nrm(ks[17], (DEPTH, SSM_GROUPS, SSM_GROUP, SSM_STATE), f32) * (2 * SSM_STATE) ** -0.5
    ssm_d = nrm(ks[18], (DEPTH, D_SSM), f32)
    ssm_w_glu = nrm(ks[19], (DEPTH, D_SSM, D_SSM), f32) * D_SSM ** -0.5
    ssm_b_glu = 0.01 * nrm(ks[20], (DEPTH, D_SSM), f32)
    lru_conv_w = nrm(ks[21], (DEPTH, CONV_W, D_LRU), f32) * CONV_W ** -0.5
    lru_conv_b = 0.01 * nrm(ks[22], (DEPTH, D_LRU), f32)
    lru_w_a = nrm(ks[23], (DEPTH, LRU_HEADS, LRU_HEAD_DIM, LRU_HEAD_DIM), f32) * LRU_HEAD_DIM ** -0.5
    lru_b_a = 0.01 * nrm(ks[24], (DEPTH, D_LRU), f32)
    lru_w_x = nrm(ks[25], (DEPTH, LRU_HEADS, LRU_HEAD_DIM, LRU_HEAD_DIM), f32) * LRU_HEAD_DIM ** -0.5
    lru_b_x = 0.01 * nrm(ks[26], (DEPTH, D_LRU), f32)
    a_target = jax.random.uniform(ks[27], (DEPTH, D_LRU), f32, 0.9, 0.999)
    s_lam = a_target ** (1.0 / RG_C)
    lru_lambda = jnp.log(s_lam) - jnp.log1p(-s_lam)
    w_attn_out = nrm(ks[28], (DEPTH, D_ATTN, D_MODEL), f32) * D_ATTN ** -0.5
    w_ssm_out = nrm(ks[29], (DEPTH, D_SSM, D_MODEL), f32) * D_SSM ** -0.5
    w_lru_out = nrm(ks[30], (DEPTH, D_LRU, D_MODEL), f32) * D_LRU ** -0.5
    w_out = nrm(ks[31], (DEPTH, D_MODEL, D_MODEL), f32) * D_MODEL ** -0.5
    w_ffn_gate = nrm(ks[32], (DEPTH, D_MODEL, D_FF), f32) * D_MODEL ** -0.5
    w_ffn_up = nrm(ks[33], (DEPTH, D_MODEL, D_FF), f32) * D_MODEL ** -0.5
    w_ffn_down = nrm(ks[34], (DEPTH, D_FF, D_MODEL), f32) * D_FF ** -0.5
    norm_pre_mix = 1.0 + 0.02 * nrm(ks[35], (DEPTH, D_MODEL), f32)
    norm_post_mix = 1.0 + 0.02 * nrm(ks[36], (DEPTH, D_MODEL), f32)
    norm_pre_ffn = 1.0 + 0.02 * nrm(ks[37], (DEPTH, D_MODEL), f32)
    norm_post_ffn = 1.0 + 0.02 * nrm(ks[38], (DEPTH, D_MODEL), f32)
    return {
        "x_prompt": x_prompt, "x_sample": x_sample,
        "cache_k": cache_k, "cache_v": cache_v, "page_table": page_table,
        "state_ssm_re": state_ssm_re, "state_ssm_im": state_ssm_im,
        "state_lru_h": state_lru_h, "state_conv": state_conv,
        "meta_tokens": meta_tokens, "w_in": w_in, "attn_bias": attn_bias,
        "ssm_a_re": ssm_a_re, "ssm_a_im": ssm_a_im, "ssm_log_step": ssm_log_step,
        "ssm_b_re": ssm_b_re, "ssm_b_im": ssm_b_im, "ssm_c_re": ssm_c_re, "ssm_c_im": ssm_c_im,
        "ssm_d": ssm_d, "ssm_w_glu": ssm_w_glu, "ssm_b_glu": ssm_b_glu,
        "lru_conv_w": lru_conv_w, "lru_conv_b": lru_conv_b,
        "lru_w_a": lru_w_a, "lru_b_a": lru_b_a, "lru_w_x": lru_w_x, "lru_b_x": lru_b_x,
        "lru_lambda": lru_lambda,
        "w_attn_out": w_attn_out, "w_ssm_out": w_ssm_out, "w_lru_out": w_lru_out, "w_out": w_out,
        "w_ffn_gate": w_ffn_gate, "w_ffn_up": w_ffn_up, "w_ffn_down": w_ffn_down,
        "norm_pre_mix": norm_pre_mix, "norm_post_mix": norm_post_mix,
        "norm_pre_ffn": norm_pre_ffn, "norm_post_ffn": norm_post_ffn,
    }


def reference(x_prompt, x_sample, cache_k, cache_v, page_table, state_ssm_re, state_ssm_im,
              state_lru_h, state_conv, meta_tokens, w_in, attn_bias, ssm_a_re, ssm_a_im, ssm_log_step,
              ssm_b_re, ssm_b_im, ssm_c_re, ssm_c_im, ssm_d, ssm_w_glu, ssm_b_glu,
              lru_conv_w, lru_conv_b, lru_w_a, lru_b_a, lru_w_x, lru_b_x, lru_lambda,
              w_attn_out, w_ssm_out, w_lru_out, w_out, w_ffn_gate, w_ffn_up, w_ffn_down,
              norm_pre_mix, norm_post_mix, norm_pre_ffn, norm_post_ffn):
    bp = x_prompt.shape[0]
    bd, t_dec, _ = x_sample.shape
    n_pages = page_table.shape[1]
    past_len = n_pages * PAGE_SIZE
    t_prompt = x_prompt.shape[1] + N_META
    dt = x_prompt.dtype
    h_p = jnp.concatenate([jnp.broadcast_to(meta_tokens.astype(dt)[None], (bp, N_META, D_MODEL)), x_prompt], axis=1)
    h_s = x_sample
    bounds_p = [0] + list(range(N_META, t_prompt, Q_BLOCK)) + [t_prompt]
    bounds_s = list(range(0, t_dec, Q_BLOCK)) + [t_dec]
    zero_ssm = jnp.zeros((bp, SSM_GROUPS, SSM_STATE), dt)
    zero_lru = jnp.zeros((bp, D_LRU), dt)
    zero_conv = jnp.zeros((bp, CONV_W - 1, D_LRU), dt)
    kp_l, vp_l, ks_l, vs_l = [], [], [], []
    srp_l, sip_l, srs_l, sis_l = [], [], [], []
    lhp_l, lhs_l, cvp_l, cvs_l = [], [], [], []
    for l in range(DEPTH):
        p = dict(w_in=w_in[l], attn_bias=attn_bias[l], ssm_a_re=ssm_a_re[l], ssm_a_im=ssm_a_im[l],
                 ssm_log_step=ssm_log_step[l],
                 ssm_b_re=ssm_b_re[l], ssm_b_im=ssm_b_im[l], ssm_c_re=ssm_c_re[l], ssm_c_im=ssm_c_im[l],
                 ssm_d=ssm_d[l], ssm_w_glu=ssm_w_glu[l], ssm_b_glu=ssm_b_glu[l],
                 lru_conv_w=lru_conv_w[l], lru_conv_b=lru_conv_b[l], lru_w_a=lru_w_a[l], lru_b_a=lru_b_a[l],
                 lru_w_x=lru_w_x[l], lru_b_x=lru_b_x[l], lru_lambda=lru_lambda[l],
                 w_attn_out=w_attn_out[l], w_ssm_out=w_ssm_out[l], w_lru_out=w_lru_out[l], w_out=w_out[l],
                 w_ffn_gate=w_ffn_gate[l], w_ffn_up=w_ffn_up[l], w_ffn_down=w_ffn_down[l],
                 norm_pre_mix=norm_pre_mix[l], norm_post_mix=norm_post_mix[l],
                 norm_pre_ffn=norm_pre_ffn[l], norm_post_ffn=norm_post_ffn[l])
        h_p, kp, vp, srp, sip, lhp, cvp = decoder_layer(h_p, p, None, None, zero_ssm, zero_ssm,
                                                        zero_lru, zero_conv, 0, bounds_p)
        past_k = cache_k[l][page_table].reshape(bd, past_len, N_HEADS, HEAD_DIM)
        past_v = cache_v[l][page_table].reshape(bd, past_len, N_HEADS, HEAD_DIM)
        h_s, ksn, vsn, srs, sis, lhs, cvs = decoder_layer(h_s, p, past_k, past_v, state_ssm_re[l], state_ssm_im[l],
                                                          state_lru_h[l], state_conv[l], past_len, bounds_s)
        kp_l.append(kp); vp_l.append(vp); ks_l.append(ksn); vs_l.append(vsn)
        srp_l.append(srp); sip_l.append(sip); srs_l.append(srs); sis_l.append(sis)
        lhp_l.append(lhp); lhs_l.append(lhs); cvp_l.append(cvp); cvs_l.append(cvs)
    y_prompt = h_p[:, N_META:]
    y_sample = h_s
    return (y_prompt, y_sample,
            jnp.stack(kp_l), jnp.stack(vp_l), jnp.stack(ks_l), jnp.stack(vs_l),
            jnp.stack(srp_l), jnp.stack(sip_l), jnp.stack(srs_l), jnp.stack(sis_l),
            jnp.stack(lhp_l), jnp.stack(lhs_l), jnp.stack(cvp_l), jnp.stack(cvs_l))
```

```python
import functools

import jax
import jax.numpy as jnp
from jax import lax
from jax.experimental import pallas as pl
from jax.experimental.pallas import tpu as pltpu

F32 = jnp.float32
BF16 = jnp.bfloat16

D_MODEL = 4096
N_META = 16
N_HEADS = 16
HEAD_DIM = 128
D_ATTN = N_HEADS * HEAD_DIM
D_SSM = 2048
SSM_GROUP = 16
SSM_GROUPS = D_SSM // SSM_GROUP
SSM_STATE = 64
D_STATE = SSM_GROUPS * SSM_STATE
D_LRU = 2048
LRU_HEADS = 16
LRU_HEAD_DIM = D_LRU // LRU_HEADS
CONV_W = 4
RG_C = 8.0
D_FF = 11008
D_FF_PAD = 11264
PAGE_SIZE = 128
EPS = 1e-6
N_BRANCH = 3

SMALL_ROWS = 32
SAMPLE_ROW0 = 16

LANE = 128
KCOL = 0
VCOL = D_ATTN // LANE
UCOL = 2 * D_ATTN // LANE
XLCOL = (2 * D_ATTN + D_SSM) // LANE
GLCOL = (2 * D_ATTN + D_SSM + D_LRU) // LANE
GATE0 = 2 * D_ATTN + D_SSM + 2 * D_LRU
N_PROJ = GATE0 + N_BRANCH * D_MODEL

SSM_TILE_GROUPS = 8
SSM_TILE_IN = SSM_TILE_GROUPS * SSM_GROUP
SSM_TILE_STATE = SSM_TILE_GROUPS * SSM_STATE
N_SSM_TILES = SSM_GROUPS // SSM_TILE_GROUPS

VMEM_LIMIT = 56 * 1024 * 1024


def _cparams(sem):
    return pltpu.CompilerParams(dimension_semantics=sem, vmem_limit_bytes=VMEM_LIMIT)


def _gelu(x):
    c = 0.7978845608028654
    return 0.5 * x * (1.0 + jnp.tanh(c * (x + 0.044715 * (x * x * x))))


def _sigmoid(x):
    return 1.0 / (1.0 + jnp.exp(-x))


def _softplus(x):
    return jnp.maximum(x, 0.0) + jnp.log1p(jnp.exp(-jnp.abs(x)))


def _rms(x, g):
    return x * lax.rsqrt(jnp.mean(x * x, axis=-1, keepdims=True) + EPS) * g


def _dot(a, b):
    return jnp.dot(a, b, preferred_element_type=F32)


def _dot_nt(a, b):
    return lax.dot_general(a, b, (((1,), (1,)), ((), ())), preferred_element_type=F32)


def _suffix_matrix(n):
    r = lax.broadcasted_iota(jnp.int32, (n, n), 0)
    c = lax.broadcasted_iota(jnp.int32, (n, n), 1)
    return jnp.where(r > c, 1.0, 0.0).astype(BF16)


def _suffix_sums(x, u):
    hi = x.astype(BF16)
    lo = (x - hi.astype(F32)).astype(BF16)
    return _dot(hi, u) + _dot(lo, u)


def _norm_kernel(x_ref, g_ref, o_ref):
    o_ref[...] = _rms(x_ref[...], g_ref[...]).astype(o_ref.dtype)


def _norm(x, g, rows):
    m, d = x.shape
    return pl.pallas_call(
        _norm_kernel,
        grid=(m // rows,),
        in_specs=[pl.BlockSpec((rows, d), lambda i: (i, 0)), pl.BlockSpec((1, d), lambda i: (0, 0))],
        out_specs=pl.BlockSpec((rows, d), lambda i: (i, 0)),
        out_shape=jax.ShapeDtypeStruct((m, d), BF16),
        compiler_params=_cparams(("parallel",)),
        name="norm",
    )(x, g.reshape(1, d))


def _resid_norm_kernel(x_ref, y_ref, gp_ref, gn_ref, xo_ref, ho_ref):
    xn = x_ref[...] + _rms(y_ref[...], gp_ref[...])
    xo_ref[...] = xn
    ho_ref[...] = _rms(xn, gn_ref[...]).astype(ho_ref.dtype)


def _resid_kernel(x_ref, y_ref, gp_ref, xo_ref):
    xo_ref[...] = x_ref[...] + _rms(y_ref[...], gp_ref[...])


def _resid_norm(x, y, g_post, g_next, rows):
    m, d = x.shape
    row_spec = pl.BlockSpec((rows, d), lambda i: (i, 0))
    g_spec = pl.BlockSpec((1, d), lambda i: (0, 0))
    if g_next is None:
        return pl.pallas_call(
            _resid_kernel,
            grid=(m // rows,),
            in_specs=[row_spec, row_spec, g_spec],
            out_specs=row_spec,
            out_shape=jax.ShapeDtypeStruct((m, d), F32),
            compiler_params=_cparams(("parallel",)),
            name="resid",
        )(x, y, g_post.reshape(1, d)), None
    return pl.pallas_call(
        _resid_norm_kernel,
        grid=(m // rows,),
        in_specs=[row_spec, row_spec, g_spec, g_spec],
        out_specs=[row_spec, row_spec],
        out_shape=[jax.ShapeDtypeStruct((m, d), F32), jax.ShapeDtypeStruct((m, d), BF16)],
        compiler_params=_cparams(("parallel",)),
        name="resid_norm",
    )(x, y, g_post.reshape(1, d), g_next.reshape(1, d))


def _mm_kernel(*refs, n_w, n_extra, epilogue, nk):
    a_ref = refs[0]
    w_refs = refs[1:1 + n_w]
    e_refs = refs[1 + n_w:1 + n_w + n_extra]
    o_ref = refs[1 + n_w + n_extra]
    acc_refs = refs[2 + n_w + n_extra:]
    k = pl.program_id(2)

    @pl.when(k == 0)
    def _():
        for acc in acc_refs:
            acc[...] = jnp.zeros_like(acc)

    a = a_ref[...]
    for w_ref, acc in zip(w_refs, acc_refs):
        acc[...] += _dot(a, w_ref[...])

    @pl.when(k == nk - 1)
    def _():
        o_ref[...] = epilogue([acc[...] for acc in acc_refs], [e[...] for e in e_refs]).astype(o_ref.dtype)


def _matmul(a, ws, *, out_dtype, tm, tn, tk, epilogue=None, extras=(), extra_specs=(), name="mm"):
    m, kdim = a.shape
    n = ws[0].shape[1]
    nk = kdim // tk
    if epilogue is None:
        epilogue = lambda accs, extras_: accs[0]
    kern = functools.partial(_mm_kernel, n_w=len(ws), n_extra=len(extras), epilogue=epilogue, nk=nk)
    return pl.pallas_call(
        kern,
        grid=(m // tm, n // tn, nk),
        in_specs=[pl.BlockSpec((tm, tk), lambda i, j, k: (i, k))]
        + [pl.BlockSpec((tk, tn), lambda i, j, k: (k, j)) for _ in ws]
        + list(extra_specs),
        out_specs=pl.BlockSpec((tm, tn), lambda i, j, k: (i, j)),
        out_shape=jax.ShapeDtypeStruct((m, n), out_dtype),
        scratch_shapes=[pltpu.VMEM((tm, tn), F32) for _ in ws],
        compiler_params=_cparams(("parallel", "parallel", "arbitrary")),
        name=name,
    )(a, *ws, *extras)


def _glu_epilogue(accs, extras):
    zg, b = extras
    return zg * _sigmoid(accs[0] + b)


def _swiglu_epilogue(accs, extras):
    g, u = accs
    return (g * _sigmoid(g)) * u


def _merge_kernel(a0_ref, a1_ref, a2_ref, w0_ref, w1_ref, w2_ref, g0_ref, g1_ref, g2_ref, o_ref, acc_ref, tot_ref,
                  *, nkb):
    k = pl.program_id(2)
    a_refs = (a0_ref, a1_ref, a2_ref)
    w_refs = (w0_ref, w1_ref, w2_ref)
    g_refs = (g0_ref, g1_ref, g2_ref)
    for br in range(N_BRANCH):
        @pl.when(k // nkb == br)
        def _(br=br):
            part = _dot(a_refs[br][...], w_refs[br][...])

            @pl.when(k % nkb == 0)
            def _():
                acc_ref[...] = part

            @pl.when(k % nkb != 0)
            def _():
                acc_ref[...] += part

            @pl.when(k % nkb == nkb - 1)
            def _():
                gated = _sigmoid(g_refs[br][...]) * acc_ref[...]
                if br == 0:
                    tot_ref[...] = gated
                elif br < N_BRANCH - 1:
                    tot_ref[...] += gated
                else:
                    o_ref[...] = (tot_ref[...] + gated).astype(o_ref.dtype)


def _merge(acts, ws, proj, *, tm, tn, tk):
    m, kdim = acts[0].shape
    n = ws[0].shape[1]
    nkb = kdim // tk
    gate_blk0 = GATE0 // tn
    nj = n // tn

    def a_spec(br):
        return pl.BlockSpec((tm, tk), lambda i, j, k: (i, jnp.clip(k - br * nkb, 0, nkb - 1)))

    def w_spec(br):
        return pl.BlockSpec((tk, tn), lambda i, j, k: (jnp.clip(k - br * nkb, 0, nkb - 1), j))

    def g_spec(br):
        return pl.BlockSpec((tm, tn), lambda i, j, k: (i, gate_blk0 + br * nj + j))

    return pl.pallas_call(
        functools.partial(_merge_kernel, nkb=nkb),
        grid=(m // tm, nj, N_BRANCH * nkb),
        in_specs=[a_spec(b) for b in range(N_BRANCH)] + [w_spec(b) for b in range(N_BRANCH)]
        + [g_spec(b) for b in range(N_BRANCH)],
        out_specs=pl.BlockSpec((tm, tn), lambda i, j, k: (i, j)),
        out_shape=jax.ShapeDtypeStruct((m, n), BF16),
        scratch_shapes=[pltpu.VMEM((tm, tn), F32), pltpu.VMEM((tm, tn), F32)],
        compiler_params=_cparams(("parallel", "parallel", "arbitrary")),
        name="merge",
    )(*acts, *ws, proj, proj, proj)


def _sb_block(qb, kb, vb, bias, u, carry, acc, mask):
    z = _dot_nt(qb, kb) * (HEAD_DIM ** -0.5) + bias
    lse = jnp.log1p(jnp.exp(-jnp.abs(z)))
    log_beta = jnp.minimum(z, 0.0) - lse
    log_keep = jnp.minimum(-z, 0.0) - lse
    if mask is not None:
        log_keep = jnp.where(mask, log_keep, 0.0)
    later = _suffix_sums(log_keep, u)
    w = jnp.exp(log_beta + later + carry)
    if mask is not None:
        w = jnp.where(mask, w, 0.0)
    acc = acc + _dot(w.astype(BF16), vb)
    carry = carry + jnp.sum(log_keep, axis=1, keepdims=True)
    return carry, acc


def _attn_meta_kernel(q_ref, k_ref, v_ref, bias_ref, o_ref):
    tq, tk = N_META, LANE
    row = lax.broadcasted_iota(jnp.int32, (tq, tk), 0)
    col = lax.broadcasted_iota(jnp.int32, (tq, tk), 1)
    carry = jnp.zeros((tq, 1), F32)
    acc = jnp.zeros((tq, HEAD_DIM), F32)
    _, acc = _sb_block(q_ref[...], k_ref[...].astype(BF16), v_ref[...].astype(BF16), bias_ref[0][:, :tk],
                       _suffix_matrix(tk), carry, acc, col < row)
    o_ref[...] = acc.astype(o_ref.dtype)


def _attn_meta(q, kpad, vpad, bias3):
    return pl.pallas_call(
        _attn_meta_kernel,
        grid=(N_HEADS,),
        in_specs=[pl.BlockSpec((N_META, HEAD_DIM), lambda h: (0, h)),
                  pl.BlockSpec((LANE, HEAD_DIM), lambda h: (0, h)),
                  pl.BlockSpec((LANE, HEAD_DIM), lambda h: (0, h)),
                  pl.BlockSpec((1, 1, 2 * LANE), lambda h: (h, 0, 0))],
        out_specs=pl.BlockSpec((N_META, HEAD_DIM), lambda h: (0, h)),
        out_shape=jax.ShapeDtypeStruct((N_META, D_ATTN), BF16),
        compiler_params=_cparams(("parallel",)),
        name="attn_meta",
    )(q, kpad, vpad, bias3)


def _attn_main_kernel(q_ref, k_ref, v_ref, kp_ref, vp_ref, bias_ref, o_ref, *, tq):
    qi = pl.program_id(2)
    bias = bias_ref[0]
    u = _suffix_matrix(tq)
    q0 = pl.multiple_of(qi * tq, tq)
    qb = q_ref[pl.ds(q0, tq), :]
    row = lax.broadcasted_iota(jnp.int32, (tq, tq), 0)
    col = lax.broadcasted_iota(jnp.int32, (tq, tq), 1)

    carry = jnp.zeros((tq, 1), F32)
    acc = jnp.zeros((tq, HEAD_DIM), F32)
    carry, acc = _sb_block(qb, k_ref[pl.ds(q0, tq), :].astype(BF16), v_ref[pl.ds(q0, tq), :].astype(BF16),
                           bias, u, carry, acc, col < row)

    def earlier(j, state):
        k0 = pl.multiple_of((qi - 1 - j) * tq, tq)
        return _sb_block(qb, k_ref[pl.ds(k0, tq), :].astype(BF16), v_ref[pl.ds(k0, tq), :].astype(BF16),
                         bias, u, state[0], state[1], None)

    carry, acc = lax.fori_loop(0, qi, earlier, (carry, acc))
    pcol = lax.broadcasted_iota(jnp.int32, (tq, LANE), 1)
    _, acc = _sb_block(qb, kp_ref[...].astype(BF16), vp_ref[...].astype(BF16), bias[:, :LANE],
                       _suffix_matrix(LANE), carry, acc, pcol < N_META)
    o_ref[pl.ds(q0, tq), :] = acc.astype(o_ref.dtype)


def _attn_main(q, proj, kpad, vpad, bias3, *, batch, t, tq):
    return pl.pallas_call(
        functools.partial(_attn_main_kernel, tq=tq),
        grid=(batch, N_HEADS, t // tq),
        in_specs=[pl.BlockSpec((t, HEAD_DIM), lambda b, h, i: (b, h)),
                  pl.BlockSpec((t, HEAD_DIM), lambda b, h, i: (b, KCOL + h)),
                  pl.BlockSpec((t, HEAD_DIM), lambda b, h, i: (b, VCOL + h)),
                  pl.BlockSpec((LANE, HEAD_DIM), lambda b, h, i: (0, h)),
                  pl.BlockSpec((LANE, HEAD_DIM), lambda b, h, i: (0, h)),
                  pl.BlockSpec((1, 1, 2 * LANE), lambda b, h, i: (h, 0, 0))],
        out_specs=pl.BlockSpec((t, HEAD_DIM), lambda b, h, i: (b, h)),
        out_shape=jax.ShapeDtypeStruct((batch * t, D_ATTN), BF16),
        compiler_params=_cparams(("parallel", "parallel", "arbitrary")),
        name="attn_main",
    )(q, proj, proj, kpad, vpad, bias3)


def _attn_decode_kernel(pt_ref, q_ref, k_ref, v_ref, bias_ref, o_ref, acc_ref, carry_ref, *, n_pages):
    j = pl.program_id(1)
    chunk = 2 * LANE
    n_chunk = PAGE_SIZE * N_HEADS // chunk

    @pl.when(j == 0)
    def _():
        acc_ref[...] = jnp.zeros_like(acc_ref)
        carry_ref[...] = jnp.zeros_like(carry_ref)

    kb = k_ref[0, 0].astype(BF16)
    z = _dot_nt(q_ref[0], kb) * (HEAD_DIM ** -0.5) + bias_ref[...]
    head = lax.broadcasted_iota(jnp.int32, z.shape, 0)
    col = lax.broadcasted_iota(jnp.int32, z.shape, 1)
    valid = (col & (N_HEADS - 1)) == head
    lse = jnp.log1p(jnp.exp(-jnp.abs(z)))
    log_beta = jnp.minimum(z, 0.0) - lse
    log_keep = jnp.where(valid, jnp.minimum(-z, 0.0) - lse, 0.0)

    stacked = jnp.concatenate([log_keep[:, c * chunk:(c + 1) * chunk] for c in range(n_chunk)], axis=0)
    within = _suffix_sums(stacked, _suffix_matrix(chunk))
    totals = jnp.sum(stacked, axis=1, keepdims=True)
    run = carry_ref[...]
    pieces = [None] * n_chunk
    for c in reversed(range(n_chunk)):
        pieces[c] = within[c * N_HEADS:(c + 1) * N_HEADS] + run
        run = run + totals[c * N_HEADS:(c + 1) * N_HEADS]
    later = jnp.concatenate(pieces, axis=1)
    w = jnp.where(valid, jnp.exp(log_beta + later), 0.0)
    acc_ref[...] += _dot(w.astype(BF16), v_ref[0, 0].astype(BF16))
    carry_ref[...] = run

    @pl.when(j == n_pages - 1)
    def _():
        o_ref[0] = acc_ref[...].astype(o_ref.dtype)


def _attn_decode(q3, cache_k4, cache_v4, page_table, bias_col, layer):
    bd, n_pages = page_table.shape
    rows = PAGE_SIZE * N_HEADS

    def page_map(b, j, pt):
        return (layer, pt[b, n_pages - 1 - j], 0, 0)

    grid_spec = pltpu.PrefetchScalarGridSpec(
        num_scalar_prefetch=1,
        grid=(bd, n_pages),
        in_specs=[pl.BlockSpec((1, N_HEADS, HEAD_DIM), lambda b, j, pt: (b, 0, 0)),
                  pl.BlockSpec((1, 1, rows, HEAD_DIM), page_map),
                  pl.BlockSpec((1, 1, rows, HEAD_DIM), page_map),
                  pl.BlockSpec((N_HEADS, 1), lambda b, j, pt: (0, 0))],
        out_specs=pl.BlockSpec((1, N_HEADS, HEAD_DIM), lambda b, j, pt: (b, 0, 0)),
        scratch_shapes=[pltpu.VMEM((N_HEADS, HEAD_DIM), F32), pltpu.VMEM((N_HEADS, 1), F32)],
    )
    return pl.pallas_call(
        functools.partial(_attn_decode_kernel, n_pages=n_pages),
        grid_spec=grid_spec,
        out_shape=jax.ShapeDtypeStruct((bd, N_HEADS, HEAD_DIM), BF16),
        compiler_params=_cparams(("parallel", "arbitrary")),
        name="attn_decode",
    )(page_table, q3, cache_k4, cache_v4, bias_col)


def _cmul(ar, ai, br, bi):
    return ar * br - ai * bi, ar * bi + ai * br


def _s5_project_out(hr, hi, u, cre_ref, cim_ref, d_ref):
    y = _dot(hr.astype(BF16), cre_ref[0]) - _dot(hi.astype(BF16), cim_ref[0])
    return _gelu(y + d_ref[...] * u)


def _s5_seq_kernel(u_ref, bre_ref, bim_ref, cre_ref, cim_ref, d_ref, ak_ref, apre_ref, apim_ref, h0re_ref, h0im_ref,
                   zg_ref, zgb_ref, hre_ref, him_ref, sre_ref, sim_ref, *, t, rc):
    ak = ak_ref[0]
    powers = ((1, ak[0:1], ak[1:2]), (2, ak[2:3], ak[3:4]), (4, ak[4:5], ak[5:6]))
    apre = apre_ref[0]
    apim = apim_ref[0]
    row = lax.broadcasted_iota(jnp.int32, (8, SSM_TILE_STATE), 0)

    def tile(i, carry):
        hpr, hpi = carry
        r0 = pl.multiple_of(i * 8, 8)
        br = sre_ref[pl.ds(r0, 8), :]
        bi = sim_ref[pl.ds(r0, 8), :]
        for k, akr, aki in powers:
            m = row >= k
            sr = jnp.where(m, pltpu.roll(br, k, 0), 0.0)
            si = jnp.where(m, pltpu.roll(bi, k, 0), 0.0)
            pr, pi = _cmul(akr, aki, sr, si)
            br = br + pr
            bi = bi + pi
        pr, pi = _cmul(apre, apim, hpr, hpi)
        br = br + pr
        bi = bi + pi
        sre_ref[pl.ds(r0, 8), :] = br
        sim_ref[pl.ds(r0, 8), :] = bi
        return br[7:8], bi[7:8]

    def chunk(c, carry):
        c0 = pl.multiple_of(c * rc, rc)
        u = u_ref[pl.ds(c0, rc), :]
        ub = u.astype(BF16)
        sre_ref[...] = _dot(ub, bre_ref[0])
        sim_ref[...] = _dot(ub, bim_ref[0])
        carry = lax.fori_loop(0, rc // 8, tile, carry)
        zg = _s5_project_out(sre_ref[...], sim_ref[...], u, cre_ref, cim_ref, d_ref)
        zg_ref[pl.ds(c0, rc), :] = zg
        zgb_ref[pl.ds(c0, rc), :] = zg.astype(BF16)
        return carry

    hre, him = lax.fori_loop(0, t // rc, chunk, (h0re_ref[0], h0im_ref[0]))
    hre_ref[0] = hre
    him_ref[0] = him


def _s5_seq(proj, sp, h0re, h0im, *, batch, t, rc):
    h0_map = (lambda b, g: (b, 0, g)) if h0re.shape[0] == batch else (lambda b, g: (0, 0, g))
    wspec_in = pl.BlockSpec((1, SSM_TILE_IN, SSM_TILE_STATE), lambda b, g: (g, 0, 0))
    wspec_out = pl.BlockSpec((1, SSM_TILE_STATE, SSM_TILE_IN), lambda b, g: (g, 0, 0))
    cspec = pl.BlockSpec((1, 8, SSM_TILE_STATE), lambda b, g: (g, 0, 0))
    sspec = pl.BlockSpec((1, 1, SSM_TILE_STATE), lambda b, g: (b, 0, g))
    act = pl.BlockSpec((t, SSM_TILE_IN), lambda b, g: (b, g))
    return pl.pallas_call(
        functools.partial(_s5_seq_kernel, t=t, rc=rc),
        grid=(batch, N_SSM_TILES),
        in_specs=[pl.BlockSpec((t, SSM_TILE_IN), lambda b, g: (b, UCOL + g)),
                  wspec_in, wspec_in, wspec_out, wspec_out,
                  pl.BlockSpec((1, SSM_TILE_IN), lambda b, g: (0, g)),
                  cspec, cspec, cspec,
                  pl.BlockSpec((1, 1, SSM_TILE_STATE), h0_map),
                  pl.BlockSpec((1, 1, SSM_TILE_STATE), h0_map)],
        out_specs=[act, act, sspec, sspec],
        out_shape=[jax.ShapeDtypeStruct((batch * t, D_SSM), F32),
                   jax.ShapeDtypeStruct((batch * t, D_SSM), BF16),
                   jax.ShapeDtypeStruct((batch, 1, D_STATE), F32),
                   jax.ShapeDtypeStruct((batch, 1, D_STATE), F32)],
        scratch_shapes=[pltpu.VMEM((rc, SSM_TILE_STATE), F32), pltpu.VMEM((rc, SSM_TILE_STATE), F32)],
        compiler_params=_cparams(("parallel", "parallel")),
        name="s5_seq",
    )(proj, sp["bre"], sp["bim"], sp["cre"], sp["cim"], sp["d"], sp["ak"], sp["apre"], sp["apim"], h0re, h0im)


def _s5_step_kernel(u_ref, bre_ref, bim_ref, cre_ref, cim_ref, d_ref, ak_ref, h0re_ref, h0im_ref,
                    zg_ref, zgb_ref, hre_ref, him_ref):
    ak = ak_ref[0]
    u = u_ref[...]
    ub = u.astype(BF16)
    pr, pi = _cmul(ak[0:1], ak[1:2], h0re_ref[...], h0im_ref[...])
    hr = pr + _dot(ub, bre_ref[0])
    hi = pi + _dot(ub, bim_ref[0])
    hre_ref[...] = hr
    him_ref[...] = hi
    zg = _s5_project_out(hr, hi, u, cre_ref, cim_ref, d_ref)
    zg_ref[...] = zg
    zgb_ref[...] = zg.astype(BF16)


def _s5_step(proj_small, sp, h0re, h0im):
    rows = h0re.shape[0]
    blk = SAMPLE_ROW0 // rows
    wspec_in = pl.BlockSpec((1, SSM_TILE_IN, SSM_TILE_STATE), lambda g: (g, 0, 0))
    wspec_out = pl.BlockSpec((1, SSM_TILE_STATE, SSM_TILE_IN), lambda g: (g, 0, 0))
    sspec = pl.BlockSpec((rows, SSM_TILE_STATE), lambda g: (0, g))
    act = pl.BlockSpec((rows, SSM_TILE_IN), lambda g: (0, g))
    return pl.pallas_call(
        _s5_step_kernel,
        grid=(N_SSM_TILES,),
        in_specs=[pl.BlockSpec((rows, SSM_TILE_IN), lambda g: (blk, UCOL + g)),
                  wspec_in, wspec_in, wspec_out, wspec_out,
                  pl.BlockSpec((1, SSM_TILE_IN), lambda g: (0, g)),
                  pl.BlockSpec((1, 8, SSM_TILE_STATE), lambda g: (g, 0, 0)),
                  sspec, sspec],
        out_specs=[act, act, sspec, sspec],
        out_shape=[jax.ShapeDtypeStruct((rows, D_SSM), F32),
                   jax.ShapeDtypeStruct((rows, D_SSM), BF16),
                   jax.ShapeDtypeStruct((rows, D_STATE), F32),
                   jax.ShapeDtypeStruct((rows, D_STATE), F32)],
        compiler_params=_cparams(("parallel",)),
        name="s5_step",
    )(proj_small, sp["bre"], sp["bim"], sp["cre"], sp["cim"], sp["d"], sp["ak"], h0re, h0im)


def _lru_gates(xf, wa_ref, ba_ref, wx_ref, bx_ref, lam_ref):
    xb = xf.astype(BF16)
    r = _sigmoid(_dot(xb, wa_ref[0]) + ba_ref[...])
    ig = _sigmoid(_dot(xb, wx_ref[0]) + bx_ref[...])
    log_a = (-RG_C * r) * _softplus(-lam_ref[...])
    a = jnp.exp(log_a)
    mult = jnp.sqrt(1.0 - jnp.exp(2.0 * log_a))
    return a, (mult * ig) * xf


def _lru_seq_kernel(x_ref, g_ref, cw_ref, cb_ref, wa_ref, ba_ref, wx_ref, bx_ref, lam_ref, cbuf_ref, h0_ref,
                    o_ref, hout_ref, cout_ref, xpad_ref, a_ref, b_ref, *, t, rc):
    pad = 8
    xpad_ref[0:pad, :] = cbuf_ref[0]
    xpad_ref[pad:pad + t, :] = x_ref[...]
    cw = cw_ref[...]
    for c in range(t // rc):
        base = c * rc
        xc = xpad_ref[base + pad - 3:base + pad - 3 + rc, :] * cw[0:1]
        for jj in range(1, CONV_W):
            xc = xc + xpad_ref[base + pad - 3 + jj:base + pad - 3 + jj + rc, :] * cw[jj:jj + 1]
        a, b = _lru_gates(xc + cb_ref[...], wa_ref, ba_ref, wx_ref, bx_ref, lam_ref)
        a_ref[base:base + rc, :] = a
        b_ref[base:base + rc, :] = b

    row = lax.broadcasted_iota(jnp.int32, (8, LRU_HEAD_DIM), 0)

    def tile(i, hp):
        r0 = pl.multiple_of(i * 8, 8)
        a = a_ref[pl.ds(r0, 8), :]
        b = b_ref[pl.ds(r0, 8), :]
        for k in (1, 2, 4):
            m = row >= k
            b = b + a * jnp.where(m, pltpu.roll(b, k, 0), 0.0)
            a = a * jnp.where(m, pltpu.roll(a, k, 0), 1.0)
        h = a * hp + b
        b_ref[pl.ds(r0, 8), :] = h
        return h[7:8]

    hout_ref[0] = lax.fori_loop(0, t // 8, tile, h0_ref[0])
    for c in range(t // rc):
        base = c * rc
        o_ref[base:base + rc, :] = (b_ref[base:base + rc, :] * _gelu(g_ref[base:base + rc, :])).astype(o_ref.dtype)
    cout_ref[0] = xpad_ref[pad + t - (CONV_W - 1):pad + t, :]


def _lru_seq(proj, lp, cbuf, h0, *, batch, t, rc):
    st_map = (lambda b, h: (b, 0, h)) if h0.shape[0] == batch else (lambda b, h: (0, 0, h))
    vec = pl.BlockSpec((1, LRU_HEAD_DIM), lambda b, h: (0, h))
    wsp = pl.BlockSpec((1, LRU_HEAD_DIM, LRU_HEAD_DIM), lambda b, h: (h, 0, 0))
    return pl.pallas_call(
        functools.partial(_lru_seq_kernel, t=t, rc=rc),
        grid=(batch, LRU_HEADS),
        in_specs=[pl.BlockSpec((t, LRU_HEAD_DIM), lambda b, h: (b, XLCOL + h)),
                  pl.BlockSpec((t, LRU_HEAD_DIM), lambda b, h: (b, GLCOL + h)),
                  pl.BlockSpec((CONV_W, LRU_HEAD_DIM), lambda b, h: (0, h)),
                  vec, wsp, vec, wsp, vec, vec,
                  pl.BlockSpec((1, 8, LRU_HEAD_DIM), st_map),
                  pl.BlockSpec((1, 1, LRU_HEAD_DIM), st_map)],
        out_specs=[pl.BlockSpec((t, LRU_HEAD_DIM), lambda b, h: (b, h)),
                   pl.BlockSpec((1, 1, LRU_HEAD_DIM), lambda b, h: (b, 0, h)),
                   pl.BlockSpec((1, CONV_W - 1, LRU_HEAD_DIM), lambda b, h: (b, 0, h))],
        out_shape=[jax.ShapeDtypeStruct((batch * t, D_LRU), BF16),
                   jax.ShapeDtypeStruct((batch, 1, D_LRU), F32),
                   jax.ShapeDtypeStruct((batch, CONV_W - 1, D_LRU), F32)],
        scratch_shapes=[pltpu.VMEM((t + 8, LRU_HEAD_DIM), F32), pltpu.VMEM((t, LRU_HEAD_DIM), F32),
                        pltpu.VMEM((t, LRU_HEAD_DIM), F32)],
        compiler_params=_cparams(("parallel", "parallel")),
        name="lru_seq",
    )(proj, proj, lp["cw"], lp["cb"], lp["wa"], lp["ba"], lp["wx"], lp["bx"], lp["lam"], cbuf, h0)


def _lru_step_kernel(x_ref, g_ref, cw_ref, cb_ref, wa_ref, ba_ref, wx_ref, bx_ref, lam_ref, cbuf_ref, h0_ref,
                     o_ref, hout_ref):
    cw = cw_ref[...]
    xc = cbuf_ref[0] * cw[0:1]
    xc = xc + cbuf_ref[1] * cw[1:2]
    xc = xc + cbuf_ref[2] * cw[2:3]
    xc = xc + x_ref[...] * cw[3:4]
    a, b = _lru_gates(xc + cb_ref[...], wa_ref, ba_ref, wx_ref, bx_ref, lam_ref)
    h = a * h0_ref[...] + b
    hout_ref[...] = h
    o_ref[...] = (h * _gelu(g_ref[...])).astype(o_ref.dtype)


def _lru_step(proj_small, lp, cbuf3, h0):
    rows = h0.shape[0]
    blk = SAMPLE_ROW0 // rows
    vec = pl.BlockSpec((1, LRU_HEAD_DIM), lambda h: (0, h))
    wsp = pl.BlockSpec((1, LRU_HEAD_DIM, LRU_HEAD_DIM), lambda h: (h, 0, 0))
    act = pl.BlockSpec((rows, LRU_HEAD_DIM), lambda h: (0, h))
    return pl.pallas_call(
        _lru_step_kernel,
        grid=(LRU_HEADS,),
        in_specs=[pl.BlockSpec((rows, LRU_HEAD_DIM), lambda h: (blk, XLCOL + h)),
                  pl.BlockSpec((rows, LRU_HEAD_DIM), lambda h: (blk, GLCOL + h)),
                  pl.BlockSpec((CONV_W, LRU_HEAD_DIM), lambda h: (0, h)),
                  vec, wsp, vec, wsp, vec, vec,
                  pl.BlockSpec((CONV_W - 1, rows, LRU_HEAD_DIM), lambda h: (0, 0, h)),
                  act],
        out_specs=[act, act],
        out_shape=[jax.ShapeDtypeStruct((rows, D_LRU), BF16), jax.ShapeDtypeStruct((rows, D_LRU), F32)],
        compiler_params=_cparams(("parallel",)),
        name="lru_step",
    )(proj_small, proj_small, lp["cw"], lp["cb"], lp["wa"], lp["ba"], lp["wx"], lp["bx"], lp["lam"], cbuf3, h0)


def _block_diag_tiles(w):
    g, r, c = w.shape
    wt = w.reshape(N_SSM_TILES, SSM_TILE_GROUPS, r, c)
    eye = jnp.eye(SSM_TILE_GROUPS, dtype=w.dtype)
    return jnp.einsum("tgrc,gh->tgrhc", wt, eye).reshape(N_SSM_TILES, SSM_TILE_GROUPS * r, SSM_TILE_GROUPS * c)


def _s5_params(lam_re, lam_im, log_step, b_re, b_im, c_re, c_im, d):
    dt = jnp.exp(log_step)[:, None]

    def apow(k):
        mag = jnp.exp(k * lam_re * dt)
        ang = k * lam_im * dt
        return (mag * jnp.cos(ang)).reshape(-1), (mag * jnp.sin(ang)).reshape(-1)

    mag = jnp.exp(lam_re * dt)
    ang = lam_im * dt
    ab_re = mag * jnp.cos(ang)
    ab_im = mag * jnp.sin(ang)
    den = lam_re * lam_re + lam_im * lam_im
    f_re = ((ab_re - 1.0) * lam_re + ab_im * lam_im) / den
    f_im = (ab_im * lam_re - (ab_re - 1.0) * lam_im) / den
    bb_re = f_re[..., None] * b_re - f_im[..., None] * b_im
    bb_im = f_re[..., None] * b_im + f_im[..., None] * b_re
    a1 = (ab_re.reshape(-1), ab_im.reshape(-1))
    a2 = apow(2.0)
    a4 = apow(4.0)
    zero = jnp.zeros_like(a1[0])
    ak = jnp.stack([a1[0], a1[1], a2[0], a2[1], a4[0], a4[1], zero, zero])
    chain = [a1] + [apow(float(k)) for k in range(2, 9)]
    apre = jnp.stack([p[0] for p in chain])
    apim = jnp.stack([p[1] for p in chain])

    def tiles(x):
        return x.reshape(8, N_SSM_TILES, SSM_TILE_STATE).transpose(1, 0, 2)

    return dict(
        bre=_block_diag_tiles(bb_re.transpose(0, 2, 1)).astype(BF16),
        bim=_block_diag_tiles(bb_im.transpose(0, 2, 1)).astype(BF16),
        cre=_block_diag_tiles(c_re.transpose(0, 2, 1)).astype(BF16),
        cim=_block_diag_tiles(c_im.transpose(0, 2, 1)).astype(BF16),
        d=d.reshape(1, D_SSM), ak=tiles(ak), apre=tiles(apre), apim=tiles(apim))


def _project_in(xn, w_q, w_rest, *, tm, tn, tk):
    q = _matmul(xn, [w_q], out_dtype=BF16, tm=tm, tn=tn, tk=tk, name="proj_q")
    proj = _matmul(xn, [w_rest], out_dtype=F32, tm=tm, tn=tn, tk=tk, name="proj_rest")
    return q, proj


def _mix_and_ffn(x, proj, o_attn, zg, zgb, o_lru, wl, g_next, *, tm, tn, tk, rows):
    m = x.shape[0]
    o_ssm = _matmul(
        zgb, [wl["w_glu"]], out_dtype=BF16, tm=tm, tn=tn, tk=tk, epilogue=_glu_epilogue,
        extras=(zg, wl["b_glu"]),
        extra_specs=(pl.BlockSpec((tm, tn), lambda i, j, k: (i, j)), pl.BlockSpec((1, tn), lambda i, j, k: (0, j))),
        name="ssm_glu")
    merged = _merge((o_attn, o_ssm, o_lru), (wl["w_attn_out"], wl["w_ssm_out"], wl["w_lru_out"]), proj,
                    tm=min(tm, 1024), tn=512, tk=tk)
    mix = _matmul(merged, [wl["w_out"]], out_dtype=F32, tm=tm, tn=tn, tk=tk, name="w_out")
    x, hn = _resid_norm(x, mix, wl["norm_post_mix"], wl["norm_pre_ffn"], rows)
    hid = _matmul(hn, [wl["w_ffn_gate"], wl["w_ffn_up"]], out_dtype=BF16, tm=tm, tn=tn, tk=tk,
                  epilogue=_swiglu_epilogue, name="ffn_in")
    ff = _matmul(hid, [wl["w_ffn_down"]], out_dtype=F32, tm=tm, tn=tn, tk=tk, name="ffn_out")
    return _resid_norm(x, ff, wl["norm_post_ffn"], g_next, rows)


def kernel(x_prompt, x_sample, cache_k, cache_v, page_table, state_ssm_re, state_ssm_im, state_lru_h, state_conv, meta_tokens, w_in, attn_bias, ssm_a_re, ssm_a_im, ssm_log_step, ssm_b_re, ssm_b_im, ssm_c_re, ssm_c_im, ssm_d, ssm_w_glu, ssm_b_glu, lru_conv_w, lru_conv_b, lru_w_a, lru_b_a, lru_w_x, lru_b_x, lru_lambda, w_attn_out, w_ssm_out, w_lru_out, w_out, w_ffn_gate, w_ffn_up, w_ffn_down, norm_pre_mix, norm_post_mix, norm_pre_ffn, norm_post_ffn):
    depth = w_in.shape[0]
    bp, t_main, _ = x_prompt.shape
    bd = x_sample.shape[0]
    n_pool = cache_k.shape[1]
    pad_rows = SMALL_ROWS - SAMPLE_ROW0 - bd
    samp_rows = SMALL_ROWS - SAMPLE_ROW0

    x_main = x_prompt.reshape(bp * t_main, D_MODEL)
    x_small = jnp.concatenate([meta_tokens, x_sample[:, 0, :], jnp.zeros((pad_rows, D_MODEL), F32)], axis=0)
    cache_k4 = cache_k.reshape(depth, n_pool, PAGE_SIZE * N_HEADS, HEAD_DIM)
    cache_v4 = cache_v.reshape(depth, n_pool, PAGE_SIZE * N_HEADS, HEAD_DIM)

    def pad_samples(x):
        return jnp.pad(x, [(0, samp_rows - bd)] + [(0, 0)] * (x.ndim - 1))

    big = dict(tm=2048, tn=1024, tk=512, rows=256)
    small = dict(tm=SMALL_ROWS, tn=1024, tk=1024, rows=SMALL_ROWS)

    xn_main = _norm(x_main, norm_pre_mix[0], big["rows"])
    xn_small = _norm(x_small, norm_pre_mix[0], small["rows"])

    outs = {name: [] for name in ("kp", "vp", "ks", "vs", "srp", "sip", "srs", "sis", "lhp", "lhs", "cvp", "cvs")}
    for l in range(depth):
        ff_pad = ((0, 0), (0, D_FF_PAD - D_FF))
        wl = dict(
            w_glu=ssm_w_glu[l].astype(BF16), b_glu=ssm_b_glu[l].reshape(1, D_SSM),
            w_attn_out=w_attn_out[l].astype(BF16), w_ssm_out=w_ssm_out[l].astype(BF16),
            w_lru_out=w_lru_out[l].astype(BF16), w_out=w_out[l].astype(BF16),
            w_ffn_gate=jnp.pad(w_ffn_gate[l].astype(BF16), ff_pad),
            w_ffn_up=jnp.pad(w_ffn_up[l].astype(BF16), ff_pad),
            w_ffn_down=jnp.pad(w_ffn_down[l].astype(BF16), ff_pad[::-1]),
            norm_post_mix=norm_post_mix[l], norm_pre_ffn=norm_pre_ffn[l], norm_post_ffn=norm_post_ffn[l])
        w_q = w_in[l][:, :D_ATTN].astype(BF16)
        w_rest = w_in[l][:, D_ATTN:].astype(BF16)
        g_next = norm_pre_mix[l + 1] if l + 1 < depth else None
        sp = _s5_params(ssm_a_re[l], ssm_a_im[l], ssm_log_step[l], ssm_b_re[l], ssm_b_im[l], ssm_c_re[l],
                        ssm_c_im[l], ssm_d[l])
        lp = dict(cw=lru_conv_w[l], cb=lru_conv_b[l].reshape(1, D_LRU), wa=lru_w_a[l].astype(BF16),
                  ba=lru_b_a[l].reshape(1, D_LRU), wx=lru_w_x[l].astype(BF16), bx=lru_b_x[l].reshape(1, D_LRU),
                  lam=lru_lambda[l].reshape(1, D_LRU))
        bias3 = jnp.broadcast_to(attn_bias[l][:, None, None], (N_HEADS, 1, 2 * LANE))
        bias_col = attn_bias[l].reshape(N_HEADS, 1)

        q_s, proj_s = _project_in(xn_small, w_q, w_rest, tm=small["tm"], tn=small["tn"], tk=small["tk"])
        k_meta = proj_s[:N_META, :D_ATTN]
        v_meta = proj_s[:N_META, D_ATTN:2 * D_ATTN]
        kpad = jnp.pad(k_meta, ((0, LANE - N_META), (0, 0)))
        vpad = jnp.pad(v_meta, ((0, LANE - N_META), (0, 0)))
        o_attn_meta = _attn_meta(q_s, kpad, vpad, bias3)
        q_dec = q_s[SAMPLE_ROW0:SAMPLE_ROW0 + bd].reshape(bd, N_HEADS, HEAD_DIM)
        o_attn_dec = _attn_decode(q_dec, cache_k4, cache_v4, page_table, bias_col, l)
        o_attn_s = jnp.concatenate([o_attn_meta, pad_samples(o_attn_dec.reshape(bd, D_ATTN))], axis=0)

        zero_state = jnp.zeros((1, 1, D_STATE), F32)
        zg_m, zgb_m, hre_m, him_m = _s5_seq(proj_s, sp, zero_state, zero_state, batch=1, t=N_META, rc=N_META)
        zg_d, zgb_d, hre_d, him_d = _s5_step(proj_s, sp, pad_samples(state_ssm_re[l].reshape(bd, D_STATE)),
                                             pad_samples(state_ssm_im[l].reshape(bd, D_STATE)))
        zg_s = jnp.concatenate([zg_m, zg_d], axis=0)
        zgb_s = jnp.concatenate([zgb_m, zgb_d], axis=0)

        o_lru_m, lh_m, cv_m = _lru_seq(proj_s, lp, jnp.zeros((1, 8, D_LRU), F32), jnp.zeros((1, 1, D_LRU), F32),
                                       batch=1, t=N_META, rc=N_META)
        cbuf_d = pad_samples(state_conv[l]).transpose(1, 0, 2)
        o_lru_d, lh_d = _lru_step(proj_s, lp, cbuf_d, pad_samples(state_lru_h[l]))
        o_lru_s = jnp.concatenate([o_lru_m, o_lru_d], axis=0)
        x_lru_d = proj_s[SAMPLE_ROW0:SAMPLE_ROW0 + bd, 2 * D_ATTN + D_SSM:2 * D_ATTN + D_SSM + D_LRU]

        x_small, xn_small = _mix_and_ffn(x_small, proj_s, o_attn_s, zg_s, zgb_s, o_lru_s, wl, g_next, **small)

        q_m, proj_m = _project_in(xn_main, w_q, w_rest, tm=big["tm"], tn=big["tn"], tk=big["tk"])
        o_attn = _attn_main(q_m, proj_m, kpad, vpad, bias3, batch=bp, t=t_main, tq=256)
        zg, zgb, hre, him = _s5_seq(proj_m, sp, hre_m, him_m, batch=bp, t=t_main, rc=256)
        cbuf_main = jnp.pad(cv_m, ((0, 0), (8 - (CONV_W - 1), 0), (0, 0)))
        o_lru, lh, cv = _lru_seq(proj_m, lp, cbuf_main, lh_m, batch=bp, t=t_main, rc=256)
        x_main, xn_main = _mix_and_ffn(x_main, proj_m, o_attn, zg, zgb, o_lru, wl, g_next, **big)

        def with_meta(main_rows, meta_rows):
            main4 = main_rows.reshape(bp, t_main, N_HEADS, HEAD_DIM)
            meta4 = jnp.broadcast_to(meta_rows.reshape(1, N_META, N_HEADS, HEAD_DIM), (bp, N_META, N_HEADS, HEAD_DIM))
            return jnp.concatenate([meta4, main4], axis=1)

        outs["kp"].append(with_meta(proj_m[:, :D_ATTN], k_meta))
        outs["vp"].append(with_meta(proj_m[:, D_ATTN:2 * D_ATTN], v_meta))
        outs["ks"].append(proj_s[SAMPLE_ROW0:SAMPLE_ROW0 + bd, :D_ATTN].reshape(bd, 1, N_HEADS, HEAD_DIM))
        outs["vs"].append(proj_s[SAMPLE_ROW0:SAMPLE_ROW0 + bd, D_ATTN:2 * D_ATTN].reshape(bd, 1, N_HEADS, HEAD_DIM))
        outs["srp"].append(hre.reshape(bp, SSM_GROUPS, SSM_STATE))
        outs["sip"].append(him.reshape(bp, SSM_GROUPS, SSM_STATE))
        outs["srs"].append(hre_d[:bd].reshape(bd, SSM_GROUPS, SSM_STATE))
        outs["sis"].append(him_d[:bd].reshape(bd, SSM_GROUPS, SSM_STATE))
        outs["lhp"].append(lh.reshape(bp, D_LRU))
        outs["lhs"].append(lh_d[:bd])
        outs["cvp"].append(cv)
        outs["cvs"].append(jnp.concatenate([state_conv[l][:, 1:], x_lru_d[:, None, :]], axis=1))

    y_prompt = x_main.reshape(bp, t_main, D_MODEL)
    y_sample = x_small[SAMPLE_ROW0:SAMPLE_ROW0 + bd].reshape(bd, 1, D_MODEL)
    st = {k: jnp.stack(v) for k, v in outs.items()}
    return (y_prompt, y_sample, st["kp"], st["vp"], st["ks"], st["vs"], st["srp"], st["sip"], st["srs"], st["sis"],
            st["lhp"], st["lhs"], st["cvp"], st["cvs"])
```

```python
import functools

import jax
import jax.numpy as jnp
from jax import lax
from jax.experimental import pallas as pl
from jax.experimental.pallas import tpu as pltpu

F32 = jnp.float32
BF16 = jnp.bfloat16

D_MODEL = 4096
N_META = 16
N_HEADS = 16
HEAD_DIM = 128
D_ATTN = N_HEADS * HEAD_DIM
D_SSM = 2048
SSM_GROUP = 16
SSM_GROUPS = D_SSM // SSM_GROUP
SSM_STATE = 64
D_STATE = SSM_GROUPS * SSM_STATE
D_LRU = 2048
LRU_HEADS = 16
LRU_HEAD_DIM = D_LRU // LRU_HEADS
CONV_W = 4
RG_C = 8.0
D_FF = 11008
D_FF_PAD = 11264
PAGE_SIZE = 128
EPS = 1e-6
N_BRANCH = 3

SMALL_ROWS = 32
SAMPLE_ROW0 = 16

LANE = 128
KCOL = 0
VCOL = D_ATTN // LANE
UCOL = 2 * D_ATTN // LANE
XLCOL = (2 * D_ATTN + D_SSM) // LANE
GLCOL = (2 * D_ATTN + D_SSM + D_LRU) // LANE
GATE0 = 2 * D_ATTN + D_SSM + 2 * D_LRU
N_PROJ = GATE0 + N_BRANCH * D_MODEL

SSM_TILE_GROUPS = 8
SSM_TILE_IN = SSM_TILE_GROUPS * SSM_GROUP
SSM_TILE_STATE = SSM_TILE_GROUPS * SSM_STATE
N_SSM_TILES = SSM_GROUPS // SSM_TILE_GROUPS

VMEM_LIMIT = 56 * 1024 * 1024


def _cparams(sem):
    return pltpu.CompilerParams(dimension_semantics=sem, vmem_limit_bytes=VMEM_LIMIT)


def _gelu(x):
    c = 0.7978845608028654
    return 0.5 * x * (1.0 + jnp.tanh(c * (x + 0.044715 * (x * x * x))))


def _sigmoid(x):
    return 1.0 / (1.0 + jnp.exp(-x))


def _softplus(x):
    return jnp.maximum(x, 0.0) + jnp.log1p(jnp.exp(-jnp.abs(x)))


def _rms(x, g):
    return x * lax.rsqrt(jnp.mean(x * x, axis=-1, keepdims=True) + EPS) * g


def _dot(a, b):
    return jnp.dot(a, b, preferred_element_type=F32)


def _dot_nt(a, b):
    return lax.dot_general(a, b, (((1,), (1,)), ((), ())), preferred_element_type=F32)


def _suffix_matrix(n):
    r = lax.broadcasted_iota(jnp.int32, (n, n), 0)
    c = lax.broadcasted_iota(jnp.int32, (n, n), 1)
    return jnp.where(r > c, 1.0, 0.0).astype(BF16)


def _suffix_sums(x, u):
    hi = x.astype(BF16)
    lo = (x - hi.astype(F32)).astype(BF16)
    return _dot(hi, u) + _dot(lo, u)


def _norm_kernel(x_ref, g_ref, o_ref):
    o_ref[...] = _rms(x_ref[...], g_ref[...]).astype(o_ref.dtype)


def _norm(x, g, rows):
    m, d = x.shape
    return pl.pallas_call(
        _norm_kernel,
        grid=(m // rows,),
        in_specs=[pl.BlockSpec((rows, d), lambda i: (i, 0)), pl.BlockSpec((1, d), lambda i: (0, 0))],
        out_specs=pl.BlockSpec((rows, d), lambda i: (i, 0)),
        out_shape=jax.ShapeDtypeStruct((m, d), BF16),
        compiler_params=_cparams(("parallel",)),
        name="norm",
    )(x, g.reshape(1, d))


def _resid_norm_kernel(x_ref, y_ref, gp_ref, gn_ref, xo_ref, ho_ref):
    xn = x_ref[...] + _rms(y_ref[...], gp_ref[...])
    xo_ref[...] = xn
    ho_ref[...] = _rms(xn, gn_ref[...]).astype(ho_ref.dtype)


def _resid_kernel(x_ref, y_ref, gp_ref, xo_ref):
    xo_ref[...] = x_ref[...] + _rms(y_ref[...], gp_ref[...])


def _resid_norm(x, y, g_post, g_next, rows):
    m, d = x.shape
    row_spec = pl.BlockSpec((rows, d), lambda i: (i, 0))
    g_spec = pl.BlockSpec((1, d), lambda i: (0, 0))
    if g_next is None:
        return pl.pallas_call(
            _resid_kernel,
            grid=(m // rows,),
            in_specs=[row_spec, row_spec, g_spec],
            out_specs=row_spec,
            out_shape=jax.ShapeDtypeStruct((m, d), F32),
            compiler_params=_cparams(("parallel",)),
            name="resid",
        )(x, y, g_post.reshape(1, d)), None
    return pl.pallas_call(
        _resid_norm_kernel,
        grid=(m // rows,),
        in_specs=[row_spec, row_spec, g_spec, g_spec],
        out_specs=[row_spec, row_spec],
        out_shape=[jax.ShapeDtypeStruct((m, d), F32), jax.ShapeDtypeStruct((m, d), BF16)],
        compiler_params=_cparams(("parallel",)),
        name="resid_norm",
    )(x, y, g_post.reshape(1, d), g_next.reshape(1, d))


def _mm_kernel(*refs, n_w, n_extra, epilogue, nk):
    a_ref = refs[0]
    w_refs = refs[1:1 + n_w]
    e_refs = refs[1 + n_w:1 + n_w + n_extra]
    o_ref = refs[1 + n_w + n_extra]
    acc_refs = refs[2 + n_w + n_extra:]
    k = pl.program_id(2)

    @pl.when(k == 0)
    def _():
        for acc in acc_refs:
            acc[...] = jnp.zeros_like(acc)

    a = a_ref[...]
    for w_ref, acc in zip(w_refs, acc_refs):
        acc[...] += _dot(a, w_ref[...].astype(BF16))

    @pl.when(k == nk - 1)
    def _():
        o_ref[...] = epilogue([acc[...] for acc in acc_refs], [e[...] for e in e_refs]).astype(o_ref.dtype)


def _weight_spec(w, layer, tk, tn, jb0=0):
    if w.ndim == 2:
        return pl.BlockSpec((tk, tn), lambda i, j, k: (k, jb0 + j))
    return pl.BlockSpec((None, tk, tn), lambda i, j, k: (layer, k, jb0 + j))


def _matmul(a, ws, *, out_dtype, tm, tn, tk, epilogue=None, extras=(), extra_specs=(), name="mm", layer=None,
            col0=0, n=None):
    m, kdim = a.shape
    n = ws[0].shape[-1] if n is None else n
    nk = kdim // tk
    if epilogue is None:
        epilogue = lambda accs, extras_: accs[0]
    kern = functools.partial(_mm_kernel, n_w=len(ws), n_extra=len(extras), epilogue=epilogue, nk=nk)
    return pl.pallas_call(
        kern,
        grid=(m // tm, n // tn, nk),
        in_specs=[pl.BlockSpec((tm, tk), lambda i, j, k: (i, k))]
        + [_weight_spec(w, layer, tk, tn, col0 // tn) for w in ws]
        + list(extra_specs),
        out_specs=pl.BlockSpec((tm, tn), lambda i, j, k: (i, j)),
        out_shape=jax.ShapeDtypeStruct((m, n), out_dtype),
        scratch_shapes=[pltpu.VMEM((tm, tn), F32) for _ in ws],
        compiler_params=_cparams(("parallel", "parallel", "arbitrary")),
        name=name,
    )(a, *ws, *extras)


def _glu_epilogue(accs, extras):
    zg, b = extras
    return zg * _sigmoid(accs[0] + b)


def _swiglu_epilogue(accs, extras):
    g, u = accs
    return (g * _sigmoid(g)) * u


def _merge_kernel(a0_ref, a1_ref, a2_ref, w0_ref, w1_ref, w2_ref, g0_ref, g1_ref, g2_ref, o_ref,
                  acc0_ref, acc1_ref, acc2_ref, *, nk):
    k = pl.program_id(2)
    accs = (acc0_ref, acc1_ref, acc2_ref)

    @pl.when(k == 0)
    def _():
        for acc in accs:
            acc[...] = jnp.zeros_like(acc)

    for a_ref, w_ref, acc in zip((a0_ref, a1_ref, a2_ref), (w0_ref, w1_ref, w2_ref), accs):
        acc[...] += _dot(a_ref[...], w_ref[...].astype(BF16))

    @pl.when(k == nk - 1)
    def _():
        merged = _sigmoid(g0_ref[...]) * acc0_ref[...] + _sigmoid(g1_ref[...]) * acc1_ref[...]
        o_ref[...] = (merged + _sigmoid(g2_ref[...]) * acc2_ref[...]).astype(o_ref.dtype)


def _merge(acts, ws, proj, *, layer, tm, tn, tk):
    m, kdim = acts[0].shape
    n = ws[0].shape[-1]
    nk = kdim // tk
    gate_blk0 = GATE0 // tn
    nj = n // tn
    a_spec = pl.BlockSpec((tm, tk), lambda i, j, k: (i, k))

    def g_spec(br):
        return pl.BlockSpec((tm, tn), lambda i, j, k: (i, gate_blk0 + br * nj + j))

    return pl.pallas_call(
        functools.partial(_merge_kernel, nk=nk),
        grid=(m // tm, nj, nk),
        in_specs=[a_spec] * N_BRANCH + [_weight_spec(w, layer, tk, tn) for w in ws]
        + [g_spec(b) for b in range(N_BRANCH)],
        out_specs=pl.BlockSpec((tm, tn), lambda i, j, k: (i, j)),
        out_shape=jax.ShapeDtypeStruct((m, n), BF16),
        scratch_shapes=[pltpu.VMEM((tm, tn), F32) for _ in range(N_BRANCH)],
        compiler_params=_cparams(("parallel", "parallel", "arbitrary")),
        name="merge",
    )(*acts, *ws, proj, proj, proj)


def _sb_blocks(qbs, kbs, vbs, biases, u, carries, accs, mask):
    n = len(qbs)
    tq = qbs[0].shape[0]
    zs = [_dot_nt(qbs[h], kbs[h]) for h in range(n)]
    log_betas, log_keeps = [], []
    for h in range(n):
        z = zs[h] * (HEAD_DIM ** -0.5) + biases[h]
        lse = jnp.log1p(jnp.exp(-jnp.abs(z)))
        log_betas.append(jnp.minimum(z, 0.0) - lse)
        log_keep = jnp.minimum(-z, 0.0) - lse
        log_keeps.append(log_keep if mask is None else jnp.where(mask, log_keep, 0.0))
    later = _suffix_sums(jnp.concatenate(log_keeps, axis=0), u)
    ws = []
    for h in range(n):
        w = jnp.exp(log_betas[h] + later[h * tq:(h + 1) * tq] + carries[h])
        ws.append((w if mask is None else jnp.where(mask, w, 0.0)).astype(BF16))
    accs = [accs[h] + _dot(ws[h], vbs[h]) for h in range(n)]
    carries = [carries[h] + jnp.sum(log_keeps[h], axis=1, keepdims=True) for h in range(n)]
    return carries, accs


def _attn_meta_kernel(q_ref, k_ref, v_ref, bias_ref, o_ref):
    tq, tk = N_META, LANE
    row = lax.broadcasted_iota(jnp.int32, (tq, tk), 0)
    col = lax.broadcasted_iota(jnp.int32, (tq, tk), 1)
    carry = jnp.zeros((tq, 1), F32)
    acc = jnp.zeros((tq, HEAD_DIM), F32)
    _, accs = _sb_blocks([q_ref[...]], [k_ref[...].astype(BF16)], [v_ref[...].astype(BF16)], [bias_ref[0][:, :tk]],
                         _suffix_matrix(tk), [carry], [acc], col < row)
    o_ref[...] = accs[0].astype(o_ref.dtype)


def _attn_meta(q, kpad, vpad, bias3):
    return pl.pallas_call(
        _attn_meta_kernel,
        grid=(N_HEADS,),
        in_specs=[pl.BlockSpec((N_META, HEAD_DIM), lambda h: (0, h)),
                  pl.BlockSpec((LANE, HEAD_DIM), lambda h: (0, h)),
                  pl.BlockSpec((LANE, HEAD_DIM), lambda h: (0, h)),
                  pl.BlockSpec((1, 1, 2 * LANE), lambda h: (h, 0, 0))],
        out_specs=pl.BlockSpec((N_META, HEAD_DIM), lambda h: (0, h)),
        out_shape=jax.ShapeDtypeStruct((N_META, D_ATTN), BF16),
        compiler_params=_cparams(("parallel",)),
        name="attn_meta",
    )(q, kpad, vpad, bias3)


ATTN_HEADS_PER_STEP = 4


def _attn_main_kernel(q_ref, k_ref, v_ref, kp_ref, vp_ref, bias_ref, o_ref, *, tq):
    qi = pl.program_id(2)
    nh = ATTN_HEADS_PER_STEP
    u = _suffix_matrix(tq)
    u_past = _suffix_matrix(LANE)
    q0 = pl.multiple_of(qi * tq, tq)
    row = lax.broadcasted_iota(jnp.int32, (tq, tq), 0)
    col = lax.broadcasted_iota(jnp.int32, (tq, tq), 1)
    pcol = lax.broadcasted_iota(jnp.int32, (tq, LANE), 1)
    lanes = [slice(h * HEAD_DIM, (h + 1) * HEAD_DIM) for h in range(nh)]
    biases = [bias_ref[h] for h in range(nh)]
    qbs = [q_ref[pl.ds(q0, tq), lanes[h]] for h in range(nh)]

    def step(k0, carries, accs, mask):
        kbs = [k_ref[pl.ds(k0, tq), lanes[h]].astype(BF16) for h in range(nh)]
        vbs = [v_ref[pl.ds(k0, tq), lanes[h]].astype(BF16) for h in range(nh)]
        return _sb_blocks(qbs, kbs, vbs, biases, u, carries, accs, mask)

    carries, accs = step(q0, [jnp.zeros((tq, 1), F32)] * nh, [jnp.zeros((tq, HEAD_DIM), F32)] * nh, col < row)

    def earlier(j, state):
        k0 = pl.multiple_of((qi - 1 - j) * tq, tq)
        carries, accs = step(k0, list(state[:nh]), list(state[nh:]), None)
        return tuple(carries) + tuple(accs)

    state = lax.fori_loop(0, qi, earlier, tuple(carries) + tuple(accs))
    _, accs = _sb_blocks(qbs, [kp_ref[:, lanes[h]].astype(BF16) for h in range(nh)],
                         [vp_ref[:, lanes[h]].astype(BF16) for h in range(nh)],
                         [b[:, :LANE] for b in biases], u_past, list(state[:nh]), list(state[nh:]), pcol < N_META)
    for h in range(nh):
        o_ref[pl.ds(q0, tq), lanes[h]] = accs[h].astype(o_ref.dtype)


def _attn_main(q, proj, kpad, vpad, bias3, *, batch, t, tq):
    nh = ATTN_HEADS_PER_STEP
    wid = nh * HEAD_DIM
    kblk = KCOL // nh
    vblk = VCOL // nh
    return pl.pallas_call(
        functools.partial(_attn_main_kernel, tq=tq),
        grid=(batch, N_HEADS // nh, t // tq),
        in_specs=[pl.BlockSpec((t, wid), lambda b, h, i: (b, h)),
                  pl.BlockSpec((t, wid), lambda b, h, i: (b, kblk + h)),
                  pl.BlockSpec((t, wid), lambda b, h, i: (b, vblk + h)),
                  pl.BlockSpec((LANE, wid), lambda b, h, i: (0, h)),
                  pl.BlockSpec((LANE, wid), lambda b, h, i: (0, h)),
                  pl.BlockSpec((nh, 1, 2 * LANE), lambda b, h, i: (h, 0, 0))],
        out_specs=pl.BlockSpec((t, wid), lambda b, h, i: (b, h)),
        out_shape=jax.ShapeDtypeStruct((batch * t, D_ATTN), BF16),
        compiler_params=_cparams(("parallel", "parallel", "arbitrary")),
        name="attn_main",
    )(q, proj, proj, kpad, vpad, bias3)


DECODE_PAGES_PER_STEP = 4


def _attn_decode_kernel(pt_ref, q_ref, *refs, n_steps):
    npp = DECODE_PAGES_PER_STEP
    k_refs = refs[:npp]
    v_refs = refs[npp:2 * npp]
    bias_ref, o_ref, acc_ref, carry_ref = refs[2 * npp:]
    j = pl.program_id(1)
    chunk = 2 * LANE
    n_chunk = PAGE_SIZE * N_HEADS // chunk

    @pl.when(j == 0)
    def _():
        acc_ref[...] = jnp.zeros_like(acc_ref)
        carry_ref[...] = jnp.zeros_like(carry_ref)

    shape = (N_HEADS, PAGE_SIZE * N_HEADS)
    head = lax.broadcasted_iota(jnp.int32, shape, 0)
    col = lax.broadcasted_iota(jnp.int32, shape, 1)
    valid = (col & (N_HEADS - 1)) == head
    u = _suffix_matrix(chunk)
    q = q_ref[0]
    bias = bias_ref[...]
    zs = [_dot_nt(q, k_refs[p][0, 0].astype(BF16)) for p in range(npp)]
    log_betas, chunks = [], []
    for p in range(npp):
        z = zs[p] * (HEAD_DIM ** -0.5) + bias
        lse = jnp.log1p(jnp.exp(-jnp.abs(z)))
        log_betas.append(jnp.minimum(z, 0.0) - lse)
        log_keep = jnp.where(valid, jnp.minimum(-z, 0.0) - lse, 0.0)
        chunks += [log_keep[:, c * chunk:(c + 1) * chunk] for c in range(n_chunk)]
    stacked = jnp.concatenate(chunks, axis=0)
    within = _suffix_sums(stacked, u)
    totals = jnp.sum(stacked, axis=1, keepdims=True)
    run = carry_ref[...]
    acc = acc_ref[...]
    for p in range(npp):
        pieces = [None] * n_chunk
        for c in reversed(range(n_chunk)):
            r0 = (p * n_chunk + c) * N_HEADS
            pieces[c] = within[r0:r0 + N_HEADS] + run
            run = run + totals[r0:r0 + N_HEADS]
        w = jnp.where(valid, jnp.exp(log_betas[p] + jnp.concatenate(pieces, axis=1)), 0.0)
        acc = acc + _dot(w.astype(BF16), v_refs[p][0, 0].astype(BF16))
    acc_ref[...] = acc
    carry_ref[...] = run

    @pl.when(j == n_steps - 1)
    def _():
        o_ref[0] = acc.astype(o_ref.dtype)


def _attn_decode(q3, cache_k4, cache_v4, page_table, bias_col, layer):
    bd, n_pages = page_table.shape
    rows = PAGE_SIZE * N_HEADS
    npp = DECODE_PAGES_PER_STEP
    n_steps = n_pages // npp

    def page_spec(p):
        return pl.BlockSpec((1, 1, rows, HEAD_DIM), lambda b, j, pt: (layer, pt[b, n_pages - 1 - (j * npp + p)], 0, 0))

    grid_spec = pltpu.PrefetchScalarGridSpec(
        num_scalar_prefetch=1,
        grid=(bd, n_steps),
        in_specs=[pl.BlockSpec((1, N_HEADS, HEAD_DIM), lambda b, j, pt: (b, 0, 0))]
        + [page_spec(p) for p in range(npp)] + [page_spec(p) for p in range(npp)]
        + [pl.BlockSpec((N_HEADS, 1), lambda b, j, pt: (0, 0))],
        out_specs=pl.BlockSpec((1, N_HEADS, HEAD_DIM), lambda b, j, pt: (b, 0, 0)),
        scratch_shapes=[pltpu.VMEM((N_HEADS, HEAD_DIM), F32), pltpu.VMEM((N_HEADS, 1), F32)],
    )
    return pl.pallas_call(
        functools.partial(_attn_decode_kernel, n_steps=n_steps),
        grid_spec=grid_spec,
        out_shape=jax.ShapeDtypeStruct((bd, N_HEADS, HEAD_DIM), BF16),
        compiler_params=_cparams(("parallel", "arbitrary")),
        name="attn_decode",
    )(page_table, q3, *([cache_k4] * npp), *([cache_v4] * npp), bias_col)


def _cmul(ar, ai, br, bi):
    return ar * br - ai * bi, ar * bi + ai * br


def _s5_project_out(hr, hi, u, cre_ref, cim_ref, d_ref):
    y = _dot(hr.astype(BF16), cre_ref[0]) - _dot(hi.astype(BF16), cim_ref[0])
    return _gelu(y + d_ref[...] * u)


def _s5_seq_kernel(u_ref, bre_ref, bim_ref, cre_ref, cim_ref, d_ref, akm_ref, apre_ref, apim_ref, h0re_ref, h0im_ref,
                   zg_ref, zgb_ref, hre_ref, him_ref, sre_ref, sim_ref, *, t, rc):
    akm = akm_ref[0]
    powers = tuple((k, akm[16 * s:16 * s + 8], akm[16 * s + 8:16 * s + 16]) for s, k in enumerate((1, 2, 4)))
    apre = apre_ref[0]
    apim = apim_ref[0]

    def tile(i, carry):
        hpr, hpi = carry
        r0 = pl.multiple_of(i * 8, 8)
        br = sre_ref[pl.ds(r0, 8), :]
        bi = sim_ref[pl.ds(r0, 8), :]
        for k, akr, aki in powers:
            pr, pi = _cmul(akr, aki, pltpu.roll(br, k, 0), pltpu.roll(bi, k, 0))
            br = br + pr
            bi = bi + pi
        pr, pi = _cmul(apre, apim, hpr, hpi)
        br = br + pr
        bi = bi + pi
        sre_ref[pl.ds(r0, 8), :] = br
        sim_ref[pl.ds(r0, 8), :] = bi
        return br[7:8], bi[7:8]

    def chunk(c, carry):
        c0 = pl.multiple_of(c * rc, rc)
        u = u_ref[pl.ds(c0, rc), :]
        ub = u.astype(BF16)
        sre_ref[...] = _dot(ub, bre_ref[0])
        sim_ref[...] = _dot(ub, bim_ref[0])
        carry = lax.fori_loop(0, rc // 8, tile, carry)
        zg = _s5_project_out(sre_ref[...], sim_ref[...], u, cre_ref, cim_ref, d_ref)
        zg_ref[pl.ds(c0, rc), :] = zg
        zgb_ref[pl.ds(c0, rc), :] = zg.astype(BF16)
        return carry

    hre, him = lax.fori_loop(0, t // rc, chunk, (h0re_ref[0], h0im_ref[0]))
    hre_ref[0] = hre
    him_ref[0] = him


def _s5_seq(proj, sp, h0re, h0im, *, batch, t, rc):
    h0_map = (lambda b, g: (b, 0, g)) if h0re.shape[0] == batch else (lambda b, g: (0, 0, g))
    wspec_in = pl.BlockSpec((1, SSM_TILE_IN, SSM_TILE_STATE), lambda b, g: (g, 0, 0))
    wspec_out = pl.BlockSpec((1, SSM_TILE_STATE, SSM_TILE_IN), lambda b, g: (g, 0, 0))
    cspec = pl.BlockSpec((1, 8, SSM_TILE_STATE), lambda b, g: (g, 0, 0))
    sspec = pl.BlockSpec((1, 1, SSM_TILE_STATE), lambda b, g: (b, 0, g))
    act = pl.BlockSpec((t, SSM_TILE_IN), lambda b, g: (b, g))
    return pl.pallas_call(
        functools.partial(_s5_seq_kernel, t=t, rc=rc),
        grid=(batch, N_SSM_TILES),
        in_specs=[pl.BlockSpec((t, SSM_TILE_IN), lambda b, g: (b, UCOL + g)),
                  wspec_in, wspec_in, wspec_out, wspec_out,
                  pl.BlockSpec((1, SSM_TILE_IN), lambda b, g: (0, g)),
                  pl.BlockSpec((1, 48, SSM_TILE_STATE), lambda b, g: (g, 0, 0)), cspec, cspec,
                  pl.BlockSpec((1, 1, SSM_TILE_STATE), h0_map),
                  pl.BlockSpec((1, 1, SSM_TILE_STATE), h0_map)],
        out_specs=[act, act, sspec, sspec],
        out_shape=[jax.ShapeDtypeStruct((batch * t, D_SSM), F32),
                   jax.ShapeDtypeStruct((batch * t, D_SSM), BF16),
                   jax.ShapeDtypeStruct((batch, 1, D_STATE), F32),
                   jax.ShapeDtypeStruct((batch, 1, D_STATE), F32)],
        scratch_shapes=[pltpu.VMEM((rc, SSM_TILE_STATE), F32), pltpu.VMEM((rc, SSM_TILE_STATE), F32)],
        compiler_params=_cparams(("parallel", "parallel")),
        name="s5_seq",
    )(proj, sp["bre"], sp["bim"], sp["cre"], sp["cim"], sp["d"], sp["akm"], sp["apre"], sp["apim"], h0re, h0im)


def _s5_step_kernel(u_ref, bre_ref, bim_ref, cre_ref, cim_ref, d_ref, ak_ref, h0re_ref, h0im_ref,
                    zg_ref, zgb_ref, hre_ref, him_ref):
    ak = ak_ref[0]
    u = u_ref[...]
    ub = u.astype(BF16)
    pr, pi = _cmul(ak[0:1], ak[1:2], h0re_ref[...], h0im_ref[...])
    hr = pr + _dot(ub, bre_ref[0])
    hi = pi + _dot(ub, bim_ref[0])
    hre_ref[...] = hr
    him_ref[...] = hi
    zg = _s5_project_out(hr, hi, u, cre_ref, cim_ref, d_ref)
    zg_ref[...] = zg
    zgb_ref[...] = zg.astype(BF16)


def _s5_step(proj_small, sp, h0re, h0im):
    rows = h0re.shape[0]
    blk = SAMPLE_ROW0 // rows
    wspec_in = pl.BlockSpec((1, SSM_TILE_IN, SSM_TILE_STATE), lambda g: (g, 0, 0))
    wspec_out = pl.BlockSpec((1, SSM_TILE_STATE, SSM_TILE_IN), lambda g: (g, 0, 0))
    sspec = pl.BlockSpec((rows, SSM_TILE_STATE), lambda g: (0, g))
    act = pl.BlockSpec((rows, SSM_TILE_IN), lambda g: (0, g))
    return pl.pallas_call(
        _s5_step_kernel,
        grid=(N_SSM_TILES,),
        in_specs=[pl.BlockSpec((rows, SSM_TILE_IN), lambda g: (blk, UCOL + g)),
                  wspec_in, wspec_in, wspec_out, wspec_out,
                  pl.BlockSpec((1, SSM_TILE_IN), lambda g: (0, g)),
                  pl.BlockSpec((1, 8, SSM_TILE_STATE), lambda g: (g, 0, 0)),
                  sspec, sspec],
        out_specs=[act, act, sspec, sspec],
        out_shape=[jax.ShapeDtypeStruct((rows, D_SSM), F32),
                   jax.ShapeDtypeStruct((rows, D_SSM), BF16),
                   jax.ShapeDtypeStruct((rows, D_STATE), F32),
                   jax.ShapeDtypeStruct((rows, D_STATE), F32)],
        compiler_params=_cparams(("parallel",)),
        name="s5_step",
    )(proj_small, sp["bre"], sp["bim"], sp["cre"], sp["cim"], sp["d"], sp["ak"], h0re, h0im)


def _lru_gates(xf, wa_ref, ba_ref, wx_ref, bx_ref, lam_ref):
    xb = xf.astype(BF16)
    r = _sigmoid(_dot(xb, wa_ref[0]) + ba_ref[...])
    ig = _sigmoid(_dot(xb, wx_ref[0]) + bx_ref[...])
    log_a = (-RG_C * r) * _softplus(-lam_ref[...])
    a = jnp.exp(log_a)
    mult = jnp.sqrt(1.0 - jnp.exp(2.0 * log_a))
    return a, (mult * ig) * xf


def _lru_seq_kernel(x_ref, g_ref, cw_ref, cb_ref, wa_ref, ba_ref, wx_ref, bx_ref, lam_ref, cbuf_ref, h0_ref,
                    o_ref, hout_ref, cout_ref, xpad_ref, a_ref, b_ref, *, t, rc):
    pad = 8
    xpad_ref[0:pad, :] = cbuf_ref[0]
    xpad_ref[pad:pad + t, :] = x_ref[...]
    cw = cw_ref[...]
    for c in range(t // rc):
        base = c * rc
        xc = xpad_ref[base + pad - 3:base + pad - 3 + rc, :] * cw[0:1]
        for jj in range(1, CONV_W):
            xc = xc + xpad_ref[base + pad - 3 + jj:base + pad - 3 + jj + rc, :] * cw[jj:jj + 1]
        a, b = _lru_gates(xc + cb_ref[...], wa_ref, ba_ref, wx_ref, bx_ref, lam_ref)
        a_ref[base:base + rc, :] = a
        b_ref[base:base + rc, :] = b

    row = lax.broadcasted_iota(jnp.int32, (8, LRU_HEAD_DIM), 0)

    def tile(i, hp):
        r0 = pl.multiple_of(i * 8, 8)
        a = a_ref[pl.ds(r0, 8), :]
        b = b_ref[pl.ds(r0, 8), :]
        for k in (1, 2, 4):
            m = row >= k
            b = b + a * jnp.where(m, pltpu.roll(b, k, 0), 0.0)
            a = a * jnp.where(m, pltpu.roll(a, k, 0), 1.0)
        h = a * hp + b
        b_ref[pl.ds(r0, 8), :] = h
        return h[7:8]

    hout_ref[0] = lax.fori_loop(0, t // 8, tile, h0_ref[0])
    for c in range(t // rc):
        base = c * rc
        o_ref[base:base + rc, :] = (b_ref[base:base + rc, :] * _gelu(g_ref[base:base + rc, :])).astype(o_ref.dtype)
    cout_ref[0] = xpad_ref[pad + t - (CONV_W - 1):pad + t, :]


def _lru_seq(proj, lp, cbuf, h0, *, batch, t, rc):
    st_map = (lambda b, h: (b, 0, h)) if h0.shape[0] == batch else (lambda b, h: (0, 0, h))
    vec = pl.BlockSpec((1, LRU_HEAD_DIM), lambda b, h: (0, h))
    wsp = pl.BlockSpec((1, LRU_HEAD_DIM, LRU_HEAD_DIM), lambda b, h: (h, 0, 0))
    return pl.pallas_call(
        functools.partial(_lru_seq_kernel, t=t, rc=rc),
        grid=(batch, LRU_HEADS),
        in_specs=[pl.BlockSpec((t, LRU_HEAD_DIM), lambda b, h: (b, XLCOL + h)),
                  pl.BlockSpec((t, LRU_HEAD_DIM), lambda b, h: (b, GLCOL + h)),
                  pl.BlockSpec((CONV_W, LRU_HEAD_DIM), lambda b, h: (0, h)),
                  vec, wsp, vec, wsp, vec, vec,
                  pl.BlockSpec((1, 8, LRU_HEAD_DIM), st_map),
                  pl.BlockSpec((1, 1, LRU_HEAD_DIM), st_map)],
        out_specs=[pl.BlockSpec((t, LRU_HEAD_DIM), lambda b, h: (b, h)),
                   pl.BlockSpec((1, 1, LRU_HEAD_DIM), lambda b, h: (b, 0, h)),
                   pl.BlockSpec((1, CONV_W - 1, LRU_HEAD_DIM), lambda b, h: (b, 0, h))],
        out_shape=[jax.ShapeDtypeStruct((batch * t, D_LRU), BF16),
                   jax.ShapeDtypeStruct((batch, 1, D_LRU), F32),
                   jax.ShapeDtypeStruct((batch, CONV_W - 1, D_LRU), F32)],
        scratch_shapes=[pltpu.VMEM((t + 8, LRU_HEAD_DIM), F32), pltpu.VMEM((t, LRU_HEAD_DIM), F32),
                        pltpu.VMEM((t, LRU_HEAD_DIM), F32)],
        compiler_params=_cparams(("parallel", "parallel")),
        name="lru_seq",
    )(proj, proj, lp["cw"], lp["cb"], lp["wa"], lp["ba"], lp["wx"], lp["bx"], lp["lam"], cbuf, h0)


def _lru_step_kernel(x_ref, g_ref, cw_ref, cb_ref, wa_ref, ba_ref, wx_ref, bx_ref, lam_ref, cbuf_ref, h0_ref,
                     o_ref, hout_ref):
    cw = cw_ref[...]
    xc = cbuf_ref[0] * cw[0:1]
    xc = xc + cbuf_ref[1] * cw[1:2]
    xc = xc + cbuf_ref[2] * cw[2:3]
    xc = xc + x_ref[...] * cw[3:4]
    a, b = _lru_gates(xc + cb_ref[...], wa_ref, ba_ref, wx_ref, bx_ref, lam_ref)
    h = a * h0_ref[...] + b
    hout_ref[...] = h
    o_ref[...] = (h * _gelu(g_ref[...])).astype(o_ref.dtype)


def _lru_step(proj_small, lp, cbuf3, h0):
    rows = h0.shape[0]
    blk = SAMPLE_ROW0 // rows
    vec = pl.BlockSpec((1, LRU_HEAD_DIM), lambda h: (0, h))
    wsp = pl.BlockSpec((1, LRU_HEAD_DIM, LRU_HEAD_DIM), lambda h: (h, 0, 0))
    act = pl.BlockSpec((rows, LRU_HEAD_DIM), lambda h: (0, h))
    return pl.pallas_call(
        _lru_step_kernel,
        grid=(LRU_HEADS,),
        in_specs=[pl.BlockSpec((rows, LRU_HEAD_DIM), lambda h: (blk, XLCOL + h)),
                  pl.BlockSpec((rows, LRU_HEAD_DIM), lambda h: (blk, GLCOL + h)),
                  pl.BlockSpec((CONV_W, LRU_HEAD_DIM), lambda h: (0, h)),
                  vec, wsp, vec, wsp, vec, vec,
                  pl.BlockSpec((CONV_W - 1, rows, LRU_HEAD_DIM), lambda h: (0, 0, h)),
                  act],
        out_specs=[act, act],
        out_shape=[jax.ShapeDtypeStruct((rows, D_LRU), BF16), jax.ShapeDtypeStruct((rows, D_LRU), F32)],
        compiler_params=_cparams(("parallel",)),
        name="lru_step",
    )(proj_small, proj_small, lp["cw"], lp["cb"], lp["wa"], lp["ba"], lp["wx"], lp["bx"], lp["lam"], cbuf3, h0)


def _block_diag_tiles(w):
    g, r, c = w.shape
    wt = w.reshape(N_SSM_TILES, SSM_TILE_GROUPS, r, c)
    eye = jnp.eye(SSM_TILE_GROUPS, dtype=w.dtype)
    return jnp.einsum("tgrc,gh->tgrhc", wt, eye).reshape(N_SSM_TILES, SSM_TILE_GROUPS * r, SSM_TILE_GROUPS * c)


def _s5_params(lam_re, lam_im, log_step, b_re, b_im, c_re, c_im, d):
    dt = jnp.exp(log_step)[:, None]

    def apow(k):
        mag = jnp.exp(k * lam_re * dt)
        ang = k * lam_im * dt
        return (mag * jnp.cos(ang)).reshape(-1), (mag * jnp.sin(ang)).reshape(-1)

    mag = jnp.exp(lam_re * dt)
    ang = lam_im * dt
    ab_re = mag * jnp.cos(ang)
    ab_im = mag * jnp.sin(ang)
    den = lam_re * lam_re + lam_im * lam_im
    f_re = ((ab_re - 1.0) * lam_re + ab_im * lam_im) / den
    f_im = (ab_im * lam_re - (ab_re - 1.0) * lam_im) / den
    bb_re = f_re[..., None] * b_re - f_im[..., None] * b_im
    bb_im = f_re[..., None] * b_im + f_im[..., None] * b_re
    a1 = (ab_re.reshape(-1), ab_im.reshape(-1))
    a2 = apow(2.0)
    a4 = apow(4.0)
    zero = jnp.zeros_like(a1[0])
    ak = jnp.stack([a1[0], a1[1], a2[0], a2[1], a4[0], a4[1], zero, zero])
    chain = [a1] + [apow(float(k)) for k in range(2, 9)]
    apre = jnp.stack([p[0] for p in chain])
    apim = jnp.stack([p[1] for p in chain])

    def tiles(x):
        return x.reshape(x.shape[0], N_SSM_TILES, SSM_TILE_STATE).transpose(1, 0, 2)

    def slab(k, x):
        return jnp.where(jnp.arange(8)[:, None] >= k, x[None, :], 0.0)

    akm = jnp.concatenate([slab(k, part) for k, a in ((1, a1), (2, a2), (4, a4)) for part in a])
    return dict(
        akm=tiles(akm),
        bre=_block_diag_tiles(bb_re.transpose(0, 2, 1)).astype(BF16),
        bim=_block_diag_tiles(bb_im.transpose(0, 2, 1)).astype(BF16),
        cre=_block_diag_tiles(c_re.transpose(0, 2, 1)).astype(BF16),
        cim=_block_diag_tiles(c_im.transpose(0, 2, 1)).astype(BF16),
        d=d.reshape(1, D_SSM), ak=tiles(ak), apre=tiles(apre), apim=tiles(apim))


def _project_in(xn, w_in, layer, *, tm, tn, tk):
    q = _matmul(xn, [w_in], out_dtype=BF16, tm=tm, tn=tn, tk=tk, name="proj_q", layer=layer, col0=0, n=D_ATTN)
    proj = _matmul(xn, [w_in], out_dtype=F32, tm=tm, tn=tn, tk=tk, name="proj_rest", layer=layer, col0=D_ATTN,
                   n=N_PROJ)
    return q, proj


def _mix_and_ffn(x, proj, o_attn, zg, zgb, o_lru, wl, layer, g_next, *, tm, tn, tk, rows):
    o_ssm = _matmul(
        zgb, [wl["w_glu"]], out_dtype=BF16, tm=tm, tn=tn, tk=512, epilogue=_glu_epilogue,
        extras=(zg, wl["b_glu"]),
        extra_specs=(pl.BlockSpec((tm, tn), lambda i, j, k: (i, j)), pl.BlockSpec((1, tn), lambda i, j, k: (0, j))),
        name="ssm_glu", layer=layer)
    merged = _merge((o_attn, o_ssm, o_lru), (wl["w_attn_out"], wl["w_ssm_out"], wl["w_lru_out"]), proj,
                    layer=layer, tm=min(tm, 1024), tn=512, tk=tk)
    mix = _matmul(merged, [wl["w_out"]], out_dtype=F32, tm=tm, tn=tn, tk=tk, name="w_out", layer=layer)
    x, hn = _resid_norm(x, mix, wl["norm_post_mix"], wl["norm_pre_ffn"], rows)
    hid = _matmul(hn, [wl["w_ffn_gate"], wl["w_ffn_up"]], out_dtype=BF16, tm=tm, tn=tn, tk=tk,
                  epilogue=_swiglu_epilogue, name="ffn_in")
    ff = _matmul(hid, [wl["w_ffn_down"]], out_dtype=F32, tm=tm, tn=tn, tk=tk, name="ffn_out")
    return _resid_norm(x, ff, wl["norm_post_ffn"], g_next, rows)


def kernel(x_prompt, x_sample, cache_k, cache_v, page_table, state_ssm_re, state_ssm_im, state_lru_h, state_conv, meta_tokens, w_in, attn_bias, ssm_a_re, ssm_a_im, ssm_log_step, ssm_b_re, ssm_b_im, ssm_c_re, ssm_c_im, ssm_d, ssm_w_glu, ssm_b_glu, lru_conv_w, lru_conv_b, lru_w_a, lru_b_a, lru_w_x, lru_b_x, lru_lambda, w_attn_out, w_ssm_out, w_lru_out, w_out, w_ffn_gate, w_ffn_up, w_ffn_down, norm_pre_mix, norm_post_mix, norm_pre_ffn, norm_post_ffn):
    depth = w_in.shape[0]
    bp, t_main, _ = x_prompt.shape
    bd = x_sample.shape[0]
    n_pool = cache_k.shape[1]
    pad_rows = SMALL_ROWS - SAMPLE_ROW0 - bd
    samp_rows = SMALL_ROWS - SAMPLE_ROW0

    x_main = x_prompt.reshape(bp * t_main, D_MODEL)
    x_small = jnp.concatenate([meta_tokens, x_sample[:, 0, :], jnp.zeros((pad_rows, D_MODEL), F32)], axis=0)
    cache_k4 = cache_k.reshape(depth, n_pool, PAGE_SIZE * N_HEADS, HEAD_DIM)
    cache_v4 = cache_v.reshape(depth, n_pool, PAGE_SIZE * N_HEADS, HEAD_DIM)

    def pad_samples(x):
        return jnp.pad(x, [(0, samp_rows - bd)] + [(0, 0)] * (x.ndim - 1))

    big = dict(tm=2048, tn=1024, tk=1024, rows=256)
    small = dict(tm=SMALL_ROWS, tn=1024, tk=1024, rows=SMALL_ROWS)

    xn_main = _norm(x_main, norm_pre_mix[0], big["rows"])
    xn_small = _norm(x_small, norm_pre_mix[0], small["rows"])

    outs = {name: [] for name in ("kp", "vp", "ks", "vs", "srp", "sip", "srs", "sis", "lhp", "lhs", "cvp", "cvs")}
    for l in range(depth):
        ff_pad = ((0, 0), (0, D_FF_PAD - D_FF))
        wl = dict(
            w_glu=ssm_w_glu, b_glu=ssm_b_glu[l].reshape(1, D_SSM),
            w_attn_out=w_attn_out, w_ssm_out=w_ssm_out, w_lru_out=w_lru_out, w_out=w_out,
            w_ffn_gate=jnp.pad(w_ffn_gate[l], ff_pad).astype(BF16),
            w_ffn_up=jnp.pad(w_ffn_up[l], ff_pad).astype(BF16),
            w_ffn_down=jnp.pad(w_ffn_down[l], ff_pad[::-1]).astype(BF16),
            norm_post_mix=norm_post_mix[l], norm_pre_ffn=norm_pre_ffn[l], norm_post_ffn=norm_post_ffn[l])
        g_next = norm_pre_mix[l + 1] if l + 1 < depth else None
        sp = _s5_params(ssm_a_re[l], ssm_a_im[l], ssm_log_step[l], ssm_b_re[l], ssm_b_im[l], ssm_c_re[l],
                        ssm_c_im[l], ssm_d[l])
        lp = dict(cw=lru_conv_w[l], cb=lru_conv_b[l].reshape(1, D_LRU), wa=lru_w_a[l].astype(BF16),
                  ba=lru_b_a[l].reshape(1, D_LRU), wx=lru_w_x[l].astype(BF16), bx=lru_b_x[l].reshape(1, D_LRU),
                  lam=lru_lambda[l].reshape(1, D_LRU))
        bias3 = jnp.broadcast_to(attn_bias[l][:, None, None], (N_HEADS, 1, 2 * LANE))
        bias_col = attn_bias[l].reshape(N_HEADS, 1)

        q_s, proj_s = _project_in(xn_small, w_in, l, tm=small["tm"], tn=small["tn"], tk=small["tk"])
        k_meta = proj_s[:N_META, :D_ATTN]
        v_meta = proj_s[:N_META, D_ATTN:2 * D_ATTN]
        kpad = jnp.pad(k_meta, ((0, LANE - N_META), (0, 0)))
        vpad = jnp.pad(v_meta, ((0, LANE - N_META), (0, 0)))
        o_attn_meta = _attn_meta(q_s, kpad, vpad, bias3)
        q_dec = q_s[SAMPLE_ROW0:SAMPLE_ROW0 + bd].reshape(bd, N_HEADS, HEAD_DIM)
        o_attn_dec = _attn_decode(q_dec, cache_k4, cache_v4, page_table, bias_col, l)
        o_attn_s = jnp.concatenate([o_attn_meta, pad_samples(o_attn_dec.reshape(bd, D_ATTN))], axis=0)

        zero_state = jnp.zeros((1, 1, D_STATE), F32)
        zg_m, zgb_m, hre_m, him_m = _s5_seq(proj_s, sp, zero_state, zero_state, batch=1, t=N_META, rc=N_META)
        zg_d, zgb_d, hre_d, him_d = _s5_step(proj_s, sp, pad_samples(state_ssm_re[l].reshape(bd, D_STATE)),
                                             pad_samples(state_ssm_im[l].reshape(bd, D_STATE)))
        zg_s = jnp.concatenate([zg_m, zg_d], axis=0)
        zgb_s = jnp.concatenate([zgb_m, zgb_d], axis=0)

        o_lru_m, lh_m, cv_m = _lru_seq(proj_s, lp, jnp.zeros((1, 8, D_LRU), F32), jnp.zeros((1, 1, D_LRU), F32),
                                       batch=1, t=N_META, rc=N_META)
        cbuf_d = pad_samples(state_conv[l]).transpose(1, 0, 2)
        o_lru_d, lh_d = _lru_step(proj_s, lp, cbuf_d, pad_samples(state_lru_h[l]))
        o_lru_s = jnp.concatenate([o_lru_m, o_lru_d], axis=0)
        x_lru_d = proj_s[SAMPLE_ROW0:SAMPLE_ROW0 + bd, 2 * D_ATTN + D_SSM:2 * D_ATTN + D_SSM + D_LRU]

        x_small, xn_small = _mix_and_ffn(x_small, proj_s, o_attn_s, zg_s, zgb_s, o_lru_s, wl, l, g_next, **small)

        q_m, proj_m = _project_in(xn_main, w_in, l, tm=big["tm"], tn=big["tn"], tk=big["tk"])
        o_attn = _attn_main(q_m, proj_m, kpad, vpad, bias3, batch=bp, t=t_main, tq=256)
        zg, zgb, hre, him = _s5_seq(proj_m, sp, hre_m, him_m, batch=bp, t=t_main, rc=256)
        cbuf_main = jnp.pad(cv_m, ((0, 0), (8 - (CONV_W - 1), 0), (0, 0)))
        o_lru, lh, cv = _lru_seq(proj_m, lp, cbuf_main, lh_m, batch=bp, t=t_main, rc=256)
        x_main, xn_main = _mix_and_ffn(x_main, proj_m, o_attn, zg, zgb, o_lru, wl, l, g_next, **big)

        def with_meta(main_rows, meta_rows):
            main4 = main_rows.reshape(bp, t_main, N_HEADS, HEAD_DIM)
            meta4 = jnp.broadcast_to(meta_rows.reshape(1, N_META, N_HEADS, HEAD_DIM), (bp, N_META, N_HEADS, HEAD_DIM))
            return jnp.concatenate([meta4, main4], axis=1)

        outs["kp"].append(with_meta(proj_m[:, :D_ATTN], k_meta))
        outs["vp"].append(with_meta(proj_m[:, D_ATTN:2 * D_ATTN], v_meta))
        outs["ks"].append(proj_s[SAMPLE_ROW0:SAMPLE_ROW0 + bd, :D_ATTN].reshape(bd, 1, N_HEADS, HEAD_DIM))
        outs["vs"].append(proj_s[SAMPLE_ROW0:SAMPLE_ROW0 + bd, D_ATTN:2 * D_ATTN].reshape(bd, 1, N_HEADS, HEAD_DIM))
        outs["srp"].append(hre.reshape(bp, SSM_GROUPS, SSM_STATE))
        outs["sip"].append(him.reshape(bp, SSM_GROUPS, SSM_STATE))
        outs["srs"].append(hre_d[:bd].reshape(bd, SSM_GROUPS, SSM_STATE))
        outs["sis"].append(him_d[:bd].reshape(bd, SSM_GROUPS, SSM_STATE))
        outs["lhp"].append(lh.reshape(bp, D_LRU))
        outs["lhs"].append(lh_d[:bd])
        outs["cvp"].append(cv)
        outs["cvs"].append(jnp.concatenate([state_conv[l][:, 1:], x_lru_d[:, None, :]], axis=1))

    y_prompt = x_main.reshape(bp, t_main, D_MODEL)
    y_sample = x_small[SAMPLE_ROW0:SAMPLE_ROW0 + bd].reshape(bd, 1, D_MODEL)
    st = {k: jnp.stack(v) for k, v in outs.items()}
    return (y_prompt, y_sample, st["kp"], st["vp"], st["ks"], st["vs"], st["srp"], st["sip"], st["srs"], st["sis"],
            st["lhp"], st["lhs"], st["cvp"], st["cvs"])
```

```python
import functools

import jax
import jax.numpy as jnp
from jax import lax
from jax.experimental import pallas as pl
from jax.experimental.pallas import tpu as pltpu

F32 = jnp.float32
BF16 = jnp.bfloat16

D_MODEL = 4096
N_META = 16
N_HEADS = 16
HEAD_DIM = 128
D_ATTN = N_HEADS * HEAD_DIM
D_SSM = 2048
SSM_GROUP = 16
SSM_GROUPS = D_SSM // SSM_GROUP
SSM_STATE = 64
D_STATE = SSM_GROUPS * SSM_STATE
D_LRU = 2048
LRU_HEADS = 16
LRU_HEAD_DIM = D_LRU // LRU_HEADS
CONV_W = 4
RG_C = 8.0
D_FF = 11008
PAGE_SIZE = 128
EPS = 1e-6
N_BRANCH = 3

SMALL_ROWS = 32
SAMPLE_ROW0 = 16

LANE = 128
KCOL = 0
VCOL = D_ATTN // LANE
UCOL = 2 * D_ATTN // LANE
XLCOL = (2 * D_ATTN + D_SSM) // LANE
GLCOL = (2 * D_ATTN + D_SSM + D_LRU) // LANE
GATE0 = 2 * D_ATTN + D_SSM + 2 * D_LRU
N_PROJ = GATE0 + N_BRANCH * D_MODEL

SSM_TILE_GROUPS = 8
SSM_TILE_IN = SSM_TILE_GROUPS * SSM_GROUP
SSM_TILE_STATE = SSM_TILE_GROUPS * SSM_STATE
N_SSM_TILES = SSM_GROUPS // SSM_TILE_GROUPS

VMEM_LIMIT = 56 * 1024 * 1024


def _cparams(sem):
    return pltpu.CompilerParams(dimension_semantics=sem, vmem_limit_bytes=VMEM_LIMIT)


def _gelu(x):
    c = 0.7978845608028654
    return 0.5 * x * (1.0 + jnp.tanh(c * (x + 0.044715 * (x * x * x))))


def _sigmoid(x):
    return 1.0 / (1.0 + jnp.exp(-x))


def _softplus(x):
    return jnp.maximum(x, 0.0) + jnp.log1p(jnp.exp(-jnp.abs(x)))


def _rms(x, g):
    return x * lax.rsqrt(jnp.mean(x * x, axis=-1, keepdims=True) + EPS) * g


def _dot(a, b):
    return jnp.dot(a, b, preferred_element_type=F32)


def _dot_nt(a, b):
    return lax.dot_general(a, b, (((1,), (1,)), ((), ())), preferred_element_type=F32)


def _suffix_matrix(n):
    r = lax.broadcasted_iota(jnp.int32, (n, n), 0)
    c = lax.broadcasted_iota(jnp.int32, (n, n), 1)
    return jnp.where(r > c, 1.0, 0.0).astype(BF16)


def _suffix_sums(x, u):
    hi = x.astype(BF16)
    lo = (x - hi.astype(F32)).astype(BF16)
    return _dot(hi, u) + _dot(lo, u)


def _norm_kernel(x_ref, g_ref, o_ref):
    o_ref[...] = _rms(x_ref[...], g_ref[...]).astype(o_ref.dtype)


def _norm(x, g, rows):
    m, d = x.shape
    return pl.pallas_call(
        _norm_kernel,
        grid=(m // rows,),
        in_specs=[pl.BlockSpec((rows, d), lambda i: (i, 0)), pl.BlockSpec((1, d), lambda i: (0, 0))],
        out_specs=pl.BlockSpec((rows, d), lambda i: (i, 0)),
        out_shape=jax.ShapeDtypeStruct((m, d), BF16),
        compiler_params=_cparams(("parallel",)),
        name="norm",
    )(x, g.reshape(1, d))


def _resid_norm_kernel(x_ref, y_ref, gp_ref, gn_ref, xo_ref, ho_ref):
    xn = x_ref[...] + _rms(y_ref[...], gp_ref[...])
    xo_ref[...] = xn
    ho_ref[...] = _rms(xn, gn_ref[...]).astype(ho_ref.dtype)


def _resid_kernel(x_ref, y_ref, gp_ref, xo_ref):
    xo_ref[...] = x_ref[...] + _rms(y_ref[...], gp_ref[...])


def _resid_norm(x, y, g_post, g_next, rows):
    m, d = x.shape
    row_spec = pl.BlockSpec((rows, d), lambda i: (i, 0))
    g_spec = pl.BlockSpec((1, d), lambda i: (0, 0))
    if g_next is None:
        return pl.pallas_call(
            _resid_kernel,
            grid=(m // rows,),
            in_specs=[row_spec, row_spec, g_spec],
            out_specs=row_spec,
            out_shape=jax.ShapeDtypeStruct((m, d), F32),
            compiler_params=_cparams(("parallel",)),
            name="resid",
        )(x, y, g_post.reshape(1, d)), None
    return pl.pallas_call(
        _resid_norm_kernel,
        grid=(m // rows,),
        in_specs=[row_spec, row_spec, g_spec, g_spec],
        out_specs=[row_spec, row_spec],
        out_shape=[jax.ShapeDtypeStruct((m, d), F32), jax.ShapeDtypeStruct((m, d), BF16)],
        compiler_params=_cparams(("parallel",)),
        name="resid_norm",
    )(x, y, g_post.reshape(1, d), g_next.reshape(1, d))


def _mm_kernel(*refs, n_w, n_extra, epilogue, nk, k_valid):
    a_ref = refs[0]
    w_refs = refs[1:1 + n_w]
    e_refs = refs[1 + n_w:1 + n_w + n_extra]
    o_ref = refs[1 + n_w + n_extra]
    acc_refs = refs[2 + n_w + n_extra:]
    k = pl.program_id(2)
    tm, tk = a_ref.shape

    @pl.when(k == 0)
    def _():
        for acc in acc_refs:
            acc[...] = jnp.zeros_like(acc)

    a = a_ref[...]
    if k_valid is not None:
        a = jnp.where(k * tk + lax.broadcasted_iota(jnp.int32, (1, tk), 1) < k_valid, a, jnp.zeros_like(a))
        w_ok = k * tk + lax.broadcasted_iota(jnp.int32, (tk, 1), 0) < k_valid
    for w_ref, acc in zip(w_refs, acc_refs):
        w = w_ref[...]
        if k_valid is not None:
            w = jnp.where(w_ok, w, jnp.zeros_like(w))
        acc[...] += _dot(a, w.astype(BF16))

    @pl.when(k == nk - 1)
    def _():
        o_ref[...] = epilogue([acc[...] for acc in acc_refs], [e[...] for e in e_refs]).astype(o_ref.dtype)


def _mm_full_k_kernel(a_ref, w_ref, o_ref):
    o_ref[...] = _dot(a_ref[...], w_ref[...].astype(BF16)).astype(o_ref.dtype)


def _swiglu_full_k_kernel(a_ref, wg_ref, wu_ref, o_ref):
    a = a_ref[...]
    g = _dot(a, wg_ref[...].astype(BF16))
    u = _dot(a, wu_ref[...].astype(BF16))
    o_ref[...] = ((g * _sigmoid(g)) * u).astype(o_ref.dtype)


def _matmul_full_k(a, ws, *, out_dtype, tm, tn, name, layer=None, col0=0, n=None):
    m, kdim = a.shape
    n = ws[0].shape[-1] if n is None else n
    w_spec = pl.BlockSpec((None, kdim, tn), lambda i, j: (layer, 0, col0 // tn + j))
    return pl.pallas_call(
        _mm_full_k_kernel if len(ws) == 1 else _swiglu_full_k_kernel,
        grid=(m // tm, pl.cdiv(n, tn)),
        in_specs=[pl.BlockSpec((tm, kdim), lambda i, j: (i, 0))] + [w_spec] * len(ws),
        out_specs=pl.BlockSpec((tm, tn), lambda i, j: (i, j)),
        out_shape=jax.ShapeDtypeStruct((m, n), out_dtype),
        compiler_params=_cparams(("parallel", "arbitrary")),
        name=name,
    )(a, *ws)


def _weight_spec(w, layer, tk, tn, jb0=0):
    if w.ndim == 2:
        return pl.BlockSpec((tk, tn), lambda i, j, k: (k, jb0 + j))
    return pl.BlockSpec((None, tk, tn), lambda i, j, k: (layer, k, jb0 + j))


def _matmul(a, ws, *, out_dtype, tm, tn, tk, epilogue=None, extras=(), extra_specs=(), name="mm", layer=None,
            col0=0, n=None):
    m, kdim = a.shape
    n = ws[0].shape[-1] if n is None else n
    nk = pl.cdiv(kdim, tk)
    if epilogue is None:
        epilogue = lambda accs, extras_: accs[0]
    kern = functools.partial(_mm_kernel, n_w=len(ws), n_extra=len(extras), epilogue=epilogue, nk=nk,
                             k_valid=None if kdim % tk == 0 else kdim)
    return pl.pallas_call(
        kern,
        grid=(m // tm, pl.cdiv(n, tn), nk),
        in_specs=[pl.BlockSpec((tm, tk), lambda i, j, k: (i, k))]
        + [_weight_spec(w, layer, tk, tn, col0 // tn) for w in ws]
        + list(extra_specs),
        out_specs=pl.BlockSpec((tm, tn), lambda i, j, k: (i, j)),
        out_shape=jax.ShapeDtypeStruct((m, n), out_dtype),
        scratch_shapes=[pltpu.VMEM((tm, tn), F32) for _ in ws],
        compiler_params=_cparams(("parallel", "parallel", "arbitrary")),
        name=name,
    )(a, *ws, *extras)


def _glu_epilogue(accs, extras):
    zg, b = extras
    return zg * _sigmoid(accs[0] + b)


def _swiglu_epilogue(accs, extras):
    g, u = accs
    return (g * _sigmoid(g)) * u


def _merge_kernel(a0_ref, a1_ref, a2_ref, w0_ref, w1_ref, w2_ref, g0_ref, g1_ref, g2_ref, o_ref,
                  acc0_ref, acc1_ref, acc2_ref, *, nk):
    k = pl.program_id(2)
    accs = (acc0_ref, acc1_ref, acc2_ref)

    @pl.when(k == 0)
    def _():
        for acc in accs:
            acc[...] = jnp.zeros_like(acc)

    for a_ref, w_ref, acc in zip((a0_ref, a1_ref, a2_ref), (w0_ref, w1_ref, w2_ref), accs):
        acc[...] += _dot(a_ref[...], w_ref[...].astype(BF16))

    @pl.when(k == nk - 1)
    def _():
        merged = _sigmoid(g0_ref[...]) * acc0_ref[...] + _sigmoid(g1_ref[...]) * acc1_ref[...]
        o_ref[...] = (merged + _sigmoid(g2_ref[...]) * acc2_ref[...]).astype(o_ref.dtype)


def _merge(acts, ws, proj, *, layer, tm, tn, tk):
    m, kdim = acts[0].shape
    n = ws[0].shape[-1]
    nk = kdim // tk
    gate_blk0 = GATE0 // tn
    nj = n // tn
    a_spec = pl.BlockSpec((tm, tk), lambda i, j, k: (i, k))

    def g_spec(br):
        return pl.BlockSpec((tm, tn), lambda i, j, k: (i, gate_blk0 + br * nj + j))

    return pl.pallas_call(
        functools.partial(_merge_kernel, nk=nk),
        grid=(m // tm, nj, nk),
        in_specs=[a_spec] * N_BRANCH + [_weight_spec(w, layer, tk, tn) for w in ws]
        + [g_spec(b) for b in range(N_BRANCH)],
        out_specs=pl.BlockSpec((tm, tn), lambda i, j, k: (i, j)),
        out_shape=jax.ShapeDtypeStruct((m, n), BF16),
        scratch_shapes=[pltpu.VMEM((tm, tn), F32) for _ in range(N_BRANCH)],
        compiler_params=_cparams(("parallel", "parallel", "arbitrary")),
        name="merge",
    )(*acts, *ws, proj, proj, proj)


def _sb_blocks(qbs, kbs, vbs, biases, u, carries, accs, mask):
    n = len(qbs)
    tq = qbs[0].shape[0]
    zs = [_dot_nt(qbs[h], kbs[h]) for h in range(n)]
    log_betas, log_keeps = [], []
    for h in range(n):
        z = zs[h] * (HEAD_DIM ** -0.5) + biases[h]
        lse = jnp.log1p(jnp.exp(-jnp.abs(z)))
        log_betas.append(jnp.minimum(z, 0.0) - lse)
        log_keep = jnp.minimum(-z, 0.0) - lse
        log_keeps.append(log_keep if mask is None else jnp.where(mask, log_keep, 0.0))
    later = _suffix_sums(jnp.concatenate(log_keeps, axis=0), u)
    ws = []
    for h in range(n):
        w = jnp.exp(log_betas[h] + later[h * tq:(h + 1) * tq] + carries[h])
        ws.append((w if mask is None else jnp.where(mask, w, 0.0)).astype(BF16))
    accs = [accs[h] + _dot(ws[h], vbs[h]) for h in range(n)]
    carries = [carries[h] + jnp.sum(log_keeps[h], axis=1, keepdims=True) for h in range(n)]
    return carries, accs


def _attn_meta_kernel(q_ref, k_ref, v_ref, bias_ref, o_ref):
    tq, tk = N_META, LANE
    row = lax.broadcasted_iota(jnp.int32, (tq, tk), 0)
    col = lax.broadcasted_iota(jnp.int32, (tq, tk), 1)
    carry = jnp.zeros((tq, 1), F32)
    acc = jnp.zeros((tq, HEAD_DIM), F32)
    _, accs = _sb_blocks([q_ref[...]], [k_ref[...].astype(BF16)], [v_ref[...].astype(BF16)], [bias_ref[0][:, :tk]],
                         _suffix_matrix(tk), [carry], [acc], col < row)
    o_ref[...] = accs[0].astype(o_ref.dtype)


def _attn_meta(q, kpad, vpad, bias3):
    return pl.pallas_call(
        _attn_meta_kernel,
        grid=(N_HEADS,),
        in_specs=[pl.BlockSpec((N_META, HEAD_DIM), lambda h: (0, h)),
                  pl.BlockSpec((LANE, HEAD_DIM), lambda h: (0, h)),
                  pl.BlockSpec((LANE, HEAD_DIM), lambda h: (0, h)),
                  pl.BlockSpec((1, 1, 2 * LANE), lambda h: (h, 0, 0))],
        out_specs=pl.BlockSpec((N_META, HEAD_DIM), lambda h: (0, h)),
        out_shape=jax.ShapeDtypeStruct((N_META, D_ATTN), BF16),
        compiler_params=_cparams(("parallel",)),
        name="attn_meta",
    )(q, kpad, vpad, bias3)


ATTN_HEADS_PER_STEP = 4


def _attn_main_kernel(q_ref, k_ref, v_ref, kp_ref, vp_ref, bias_ref, o_ref, *, tq):
    qi = pl.program_id(2)
    nh = ATTN_HEADS_PER_STEP
    u = _suffix_matrix(tq)
    u_past = _suffix_matrix(LANE)
    q0 = pl.multiple_of(qi * tq, tq)
    row = lax.broadcasted_iota(jnp.int32, (tq, tq), 0)
    col = lax.broadcasted_iota(jnp.int32, (tq, tq), 1)
    pcol = lax.broadcasted_iota(jnp.int32, (tq, LANE), 1)
    lanes = [slice(h * HEAD_DIM, (h + 1) * HEAD_DIM) for h in range(nh)]
    biases = [bias_ref[h] for h in range(nh)]
    qbs = [q_ref[pl.ds(q0, tq), lanes[h]] for h in range(nh)]

    def step(k0, carries, accs, mask):
        kbs = [k_ref[pl.ds(k0, tq), lanes[h]].astype(BF16) for h in range(nh)]
        vbs = [v_ref[pl.ds(k0, tq), lanes[h]].astype(BF16) for h in range(nh)]
        return _sb_blocks(qbs, kbs, vbs, biases, u, carries, accs, mask)

    carries, accs = step(q0, [jnp.zeros((tq, 1), F32)] * nh, [jnp.zeros((tq, HEAD_DIM), F32)] * nh, col < row)

    def earlier(j, state):
        k0 = pl.multiple_of((qi - 1 - j) * tq, tq)
        carries, accs = step(k0, list(state[:nh]), list(state[nh:]), None)
        return tuple(carries) + tuple(accs)

    state = lax.fori_loop(0, qi, earlier, tuple(carries) + tuple(accs))
    _, accs = _sb_blocks(qbs, [kp_ref[:, lanes[h]].astype(BF16) for h in range(nh)],
                         [vp_ref[:, lanes[h]].astype(BF16) for h in range(nh)],
                         [b[:, :LANE] for b in biases], u_past, list(state[:nh]), list(state[nh:]), pcol < N_META)
    for h in range(nh):
        o_ref[pl.ds(q0, tq), lanes[h]] = accs[h].astype(o_ref.dtype)


def _attn_main(q, proj, kpad, vpad, bias3, *, batch, t, tq):
    nh = ATTN_HEADS_PER_STEP
    wid = nh * HEAD_DIM
    kblk = KCOL // nh
    vblk = VCOL // nh
    return pl.pallas_call(
        functools.partial(_attn_main_kernel, tq=tq),
        grid=(batch, N_HEADS // nh, t // tq),
        in_specs=[pl.BlockSpec((t, wid), lambda b, h, i: (b, h)),
                  pl.BlockSpec((t, wid), lambda b, h, i: (b, kblk + h)),
                  pl.BlockSpec((t, wid), lambda b, h, i: (b, vblk + h)),
                  pl.BlockSpec((LANE, wid), lambda b, h, i: (0, h)),
                  pl.BlockSpec((LANE, wid), lambda b, h, i: (0, h)),
                  pl.BlockSpec((nh, 1, 2 * LANE), lambda b, h, i: (h, 0, 0))],
        out_specs=pl.BlockSpec((t, wid), lambda b, h, i: (b, h)),
        out_shape=jax.ShapeDtypeStruct((batch * t, D_ATTN), BF16),
        compiler_params=_cparams(("parallel", "parallel", "arbitrary")),
        name="attn_main",
    )(q, proj, proj, kpad, vpad, bias3)


DECODE_PAGES_PER_STEP = 4


def _attn_decode_kernel(pt_ref, q_ref, *refs, n_steps):
    npp = DECODE_PAGES_PER_STEP
    k_refs = refs[:npp]
    v_refs = refs[npp:2 * npp]
    bias_ref, o_ref, acc_ref, carry_ref = refs[2 * npp:]
    j = pl.program_id(1)
    chunk = 2 * LANE
    n_chunk = PAGE_SIZE * N_HEADS // chunk

    @pl.when(j == 0)
    def _():
        acc_ref[...] = jnp.zeros_like(acc_ref)
        carry_ref[...] = jnp.zeros_like(carry_ref)

    shape = (N_HEADS, PAGE_SIZE * N_HEADS)
    head = lax.broadcasted_iota(jnp.int32, shape, 0)
    col = lax.broadcasted_iota(jnp.int32, shape, 1)
    valid = (col & (N_HEADS - 1)) == head
    u = _suffix_matrix(chunk)
    q = q_ref[0]
    bias = bias_ref[...]
    zs = [_dot_nt(q, k_refs[p][0, 0].astype(BF16)) for p in range(npp)]
    log_betas, chunks = [], []
    for p in range(npp):
        z = zs[p] * (HEAD_DIM ** -0.5) + bias
        lse = jnp.log1p(jnp.exp(-jnp.abs(z)))
        log_betas.append(jnp.minimum(z, 0.0) - lse)
        log_keep = jnp.where(valid, jnp.minimum(-z, 0.0) - lse, 0.0)
        chunks += [log_keep[:, c * chunk:(c + 1) * chunk] for c in range(n_chunk)]
    stacked = jnp.concatenate(chunks, axis=0)
    within = _suffix_sums(stacked, u)
    totals = jnp.sum(stacked, axis=1, keepdims=True)
    run = carry_ref[...]
    acc = acc_ref[...]
    for p in range(npp):
        pieces = [None] * n_chunk
        for c in reversed(range(n_chunk)):
            r0 = (p * n_chunk + c) * N_HEADS
            pieces[c] = within[r0:r0 + N_HEADS] + run
            run = run + totals[r0:r0 + N_HEADS]
        w = jnp.where(valid, jnp.exp(log_betas[p] + jnp.concatenate(pieces, axis=1)), 0.0)
        acc = acc + _dot(w.astype(BF16), v_refs[p][0, 0].astype(BF16))
    acc_ref[...] = acc
    carry_ref[...] = run

    @pl.when(j == n_steps - 1)
    def _():
        o_ref[0] = acc.astype(o_ref.dtype)


def _attn_decode(q3, cache_k4, cache_v4, page_table, bias_col, layer):
    bd, n_pages = page_table.shape
    rows = PAGE_SIZE * N_HEADS
    npp = DECODE_PAGES_PER_STEP
    n_steps = n_pages // npp

    def page_spec(p):
        return pl.BlockSpec((1, 1, rows, HEAD_DIM), lambda b, j, pt: (layer, pt[b, n_pages - 1 - (j * npp + p)], 0, 0))

    grid_spec = pltpu.PrefetchScalarGridSpec(
        num_scalar_prefetch=1,
        grid=(bd, n_steps),
        in_specs=[pl.BlockSpec((1, N_HEADS, HEAD_DIM), lambda b, j, pt: (b, 0, 0))]
        + [page_spec(p) for p in range(npp)] + [page_spec(p) for p in range(npp)]
        + [pl.BlockSpec((N_HEADS, 1), lambda b, j, pt: (0, 0))],
        out_specs=pl.BlockSpec((1, N_HEADS, HEAD_DIM), lambda b, j, pt: (b, 0, 0)),
        scratch_shapes=[pltpu.VMEM((N_HEADS, HEAD_DIM), F32), pltpu.VMEM((N_HEADS, 1), F32)],
    )
    return pl.pallas_call(
        functools.partial(_attn_decode_kernel, n_steps=n_steps),
        grid_spec=grid_spec,
        out_shape=jax.ShapeDtypeStruct((bd, N_HEADS, HEAD_DIM), BF16),
        compiler_params=_cparams(("parallel", "arbitrary")),
        name="attn_decode",
    )(page_table, q3, *([cache_k4] * npp), *([cache_v4] * npp), bias_col)


def _cmul(ar, ai, br, bi):
    return ar * br - ai * bi, ar * bi + ai * br


def _s5_project_out(hr, hi, u, cre_ref, cim_ref, d_ref):
    y = _dot(hr.astype(BF16), cre_ref[0]) - _dot(hi.astype(BF16), cim_ref[0])
    return _gelu(y + d_ref[...] * u)


def _s5_seq_kernel(u_ref, bre_ref, bim_ref, cre_ref, cim_ref, d_ref, akm_ref, apre_ref, apim_ref, h0re_ref, h0im_ref,
                   zg_ref, zgb_ref, hre_ref, him_ref, sre_ref, sim_ref, *, t, rc):
    akm = akm_ref[0]
    powers = tuple((k, akm[16 * s:16 * s + 8], akm[16 * s + 8:16 * s + 16]) for s, k in enumerate((1, 2, 4)))
    apre = apre_ref[0]
    apim = apim_ref[0]

    def tile(i, carry):
        hpr, hpi = carry
        r0 = pl.multiple_of(i * 8, 8)
        br = sre_ref[pl.ds(r0, 8), :]
        bi = sim_ref[pl.ds(r0, 8), :]
        for k, akr, aki in powers:
            pr, pi = _cmul(akr, aki, pltpu.roll(br, k, 0), pltpu.roll(bi, k, 0))
            br = br + pr
            bi = bi + pi
        pr, pi = _cmul(apre, apim, hpr, hpi)
        br = br + pr
        bi = bi + pi
        sre_ref[pl.ds(r0, 8), :] = br
        sim_ref[pl.ds(r0, 8), :] = bi
        return br[7:8], bi[7:8]

    def chunk(c, carry):
        c0 = pl.multiple_of(c * rc, rc)
        u = u_ref[pl.ds(c0, rc), :]
        ub = u.astype(BF16)
        sre_ref[...] = _dot(ub, bre_ref[0])
        sim_ref[...] = _dot(ub, bim_ref[0])
        carry = lax.fori_loop(0, rc // 8, tile, carry)
        zg = _s5_project_out(sre_ref[...], sim_ref[...], u, cre_ref, cim_ref, d_ref)
        zg_ref[pl.ds(c0, rc), :] = zg
        zgb_ref[pl.ds(c0, rc), :] = zg.astype(BF16)
        return carry

    hre, him = lax.fori_loop(0, t // rc, chunk, (h0re_ref[0], h0im_ref[0]))
    hre_ref[0] = hre
    him_ref[0] = him


def _s5_seq(proj, sp, h0re, h0im, *, batch, t, rc):
    h0_map = (lambda b, g: (b, 0, g)) if h0re.shape[0] == batch else (lambda b, g: (0, 0, g))
    wspec_in = pl.BlockSpec((1, SSM_TILE_IN, SSM_TILE_STATE), lambda b, g: (g, 0, 0))
    wspec_out = pl.BlockSpec((1, SSM_TILE_STATE, SSM_TILE_IN), lambda b, g: (g, 0, 0))
    cspec = pl.BlockSpec((1, 8, SSM_TILE_STATE), lambda b, g: (g, 0, 0))
    sspec = pl.BlockSpec((1, 1, SSM_TILE_STATE), lambda b, g: (b, 0, g))
    act = pl.BlockSpec((t, SSM_TILE_IN), lambda b, g: (b, g))
    return pl.pallas_call(
        functools.partial(_s5_seq_kernel, t=t, rc=rc),
        grid=(batch, N_SSM_TILES),
        in_specs=[pl.BlockSpec((t, SSM_TILE_IN), lambda b, g: (b, UCOL + g)),
                  wspec_in, wspec_in, wspec_out, wspec_out,
                  pl.BlockSpec((1, SSM_TILE_IN), lambda b, g: (0, g)),
                  pl.BlockSpec((1, 48, SSM_TILE_STATE), lambda b, g: (g, 0, 0)), cspec, cspec,
                  pl.BlockSpec((1, 1, SSM_TILE_STATE), h0_map),
                  pl.BlockSpec((1, 1, SSM_TILE_STATE), h0_map)],
        out_specs=[act, act, sspec, sspec],
        out_shape=[jax.ShapeDtypeStruct((batch * t, D_SSM), F32),
                   jax.ShapeDtypeStruct((batch * t, D_SSM), BF16),
                   jax.ShapeDtypeStruct((batch, 1, D_STATE), F32),
                   jax.ShapeDtypeStruct((batch, 1, D_STATE), F32)],
        scratch_shapes=[pltpu.VMEM((rc, SSM_TILE_STATE), F32), pltpu.VMEM((rc, SSM_TILE_STATE), F32)],
        compiler_params=_cparams(("parallel", "parallel")),
        name="s5_seq",
    )(proj, sp["bre"], sp["bim"], sp["cre"], sp["cim"], sp["d"], sp["akm"], sp["apre"], sp["apim"], h0re, h0im)


def _s5_step_kernel(u_ref, bre_ref, bim_ref, cre_ref, cim_ref, d_ref, ak_ref, h0re_ref, h0im_ref,
                    zg_ref, zgb_ref, hre_ref, him_ref):
    ak = ak_ref[0]
    u = u_ref[...]
    ub = u.astype(BF16)
    pr, pi = _cmul(ak[0:1], ak[1:2], h0re_ref[...], h0im_ref[...])
    hr = pr + _dot(ub, bre_ref[0])
    hi = pi + _dot(ub, bim_ref[0])
    hre_ref[...] = hr
    him_ref[...] = hi
    zg = _s5_project_out(hr, hi, u, cre_ref, cim_ref, d_ref)
    zg_ref[...] = zg
    zgb_ref[...] = zg.astype(BF16)


def _s5_step(proj_small, sp, h0re, h0im):
    rows = h0re.shape[0]
    blk = SAMPLE_ROW0 // rows
    wspec_in = pl.BlockSpec((1, SSM_TILE_IN, SSM_TILE_STATE), lambda g: (g, 0, 0))
    wspec_out = pl.BlockSpec((1, SSM_TILE_STATE, SSM_TILE_IN), lambda g: (g, 0, 0))
    sspec = pl.BlockSpec((rows, SSM_TILE_STATE), lambda g: (0, g))
    act = pl.BlockSpec((rows, SSM_TILE_IN), lambda g: (0, g))
    return pl.pallas_call(
        _s5_step_kernel,
        grid=(N_SSM_TILES,),
        in_specs=[pl.BlockSpec((rows, SSM_TILE_IN), lambda g: (blk, UCOL + g)),
                  wspec_in, wspec_in, wspec_out, wspec_out,
                  pl.BlockSpec((1, SSM_TILE_IN), lambda g: (0, g)),
                  pl.BlockSpec((1, 8, SSM_TILE_STATE), lambda g: (g, 0, 0)),
                  sspec, sspec],
        out_specs=[act, act, sspec, sspec],
        out_shape=[jax.ShapeDtypeStruct((rows, D_SSM), F32),
                   jax.ShapeDtypeStruct((rows, D_SSM), BF16),
                   jax.ShapeDtypeStruct((rows, D_STATE), F32),
                   jax.ShapeDtypeStruct((rows, D_STATE), F32)],
        compiler_params=_cparams(("parallel",)),
        name="s5_step",
    )(proj_small, sp["bre"], sp["bim"], sp["cre"], sp["cim"], sp["d"], sp["ak"], h0re, h0im)


def _lru_gates(xf, wa_ref, ba_ref, wx_ref, bx_ref, lam_ref):
    xb = xf.astype(BF16)
    r = _sigmoid(_dot(xb, wa_ref[0]) + ba_ref[...])
    ig = _sigmoid(_dot(xb, wx_ref[0]) + bx_ref[...])
    log_a = (-RG_C * r) * _softplus(-lam_ref[...])
    a = jnp.exp(log_a)
    mult = jnp.sqrt(1.0 - jnp.exp(2.0 * log_a))
    return a, (mult * ig) * xf


def _lru_seq_kernel(x_ref, g_ref, cw_ref, cb_ref, wa_ref, ba_ref, wx_ref, bx_ref, lam_ref, cbuf_ref, h0_ref,
                    o_ref, hout_ref, cout_ref, xpad_ref, a_ref, b_ref, *, t, rc):
    pad = 8
    xpad_ref[0:pad, :] = cbuf_ref[0]
    xpad_ref[pad:pad + t, :] = x_ref[...]
    cw = cw_ref[...]
    for c in range(t // rc):
        base = c * rc
        xc = xpad_ref[base + pad - 3:base + pad - 3 + rc, :] * cw[0:1]
        for jj in range(1, CONV_W):
            xc = xc + xpad_ref[base + pad - 3 + jj:base + pad - 3 + jj + rc, :] * cw[jj:jj + 1]
        a, b = _lru_gates(xc + cb_ref[...], wa_ref, ba_ref, wx_ref, bx_ref, lam_ref)
        a_ref[base:base + rc, :] = a
        b_ref[base:base + rc, :] = b

    row = lax.broadcasted_iota(jnp.int32, (8, LRU_HEAD_DIM), 0)

    def tile(i, hp):
        r0 = pl.multiple_of(i * 8, 8)
        a = a_ref[pl.ds(r0, 8), :]
        b = b_ref[pl.ds(r0, 8), :]
        for k in (1, 2, 4):
            m = row >= k
            b = b + a * jnp.where(m, pltpu.roll(b, k, 0), 0.0)
            a = a * jnp.where(m, pltpu.roll(a, k, 0), 1.0)
        h = a * hp + b
        b_ref[pl.ds(r0, 8), :] = h
        return h[7:8]

    hout_ref[0] = lax.fori_loop(0, t // 8, tile, h0_ref[0])
    for c in range(t // rc):
        base = c * rc
        o_ref[base:base + rc, :] = (b_ref[base:base + rc, :] * _gelu(g_ref[base:base + rc, :])).astype(o_ref.dtype)
    cout_ref[0] = xpad_ref[pad + t - (CONV_W - 1):pad + t, :]


def _lru_seq(proj, lp, cbuf, h0, *, batch, t, rc):
    st_map = (lambda b, h: (b, 0, h)) if h0.shape[0] == batch else (lambda b, h: (0, 0, h))
    vec = pl.BlockSpec((1, LRU_HEAD_DIM), lambda b, h: (0, h))
    wsp = pl.BlockSpec((1, LRU_HEAD_DIM, LRU_HEAD_DIM), lambda b, h: (h, 0, 0))
    return pl.pallas_call(
        functools.partial(_lru_seq_kernel, t=t, rc=rc),
        grid=(batch, LRU_HEADS),
        in_specs=[pl.BlockSpec((t, LRU_HEAD_DIM), lambda b, h: (b, XLCOL + h)),
                  pl.BlockSpec((t, LRU_HEAD_DIM), lambda b, h: (b, GLCOL + h)),
                  pl.BlockSpec((CONV_W, LRU_HEAD_DIM), lambda b, h: (0, h)),
                  vec, wsp, vec, wsp, vec, vec,
                  pl.BlockSpec((1, 8, LRU_HEAD_DIM), st_map),
                  pl.BlockSpec((1, 1, LRU_HEAD_DIM), st_map)],
        out_specs=[pl.BlockSpec((t, LRU_HEAD_DIM), lambda b, h: (b, h)),
                   pl.BlockSpec((1, 1, LRU_HEAD_DIM), lambda b, h: (b, 0, h)),
                   pl.BlockSpec((1, CONV_W - 1, LRU_HEAD_DIM), lambda b, h: (b, 0, h))],
        out_shape=[jax.ShapeDtypeStruct((batch * t, D_LRU), BF16),
                   jax.ShapeDtypeStruct((batch, 1, D_LRU), F32),
                   jax.ShapeDtypeStruct((batch, CONV_W - 1, D_LRU), F32)],
        scratch_shapes=[pltpu.VMEM((t + 8, LRU_HEAD_DIM), F32), pltpu.VMEM((t, LRU_HEAD_DIM), F32),
                        pltpu.VMEM((t, LRU_HEAD_DIM), F32)],
        compiler_params=_cparams(("parallel", "parallel")),
        name="lru_seq",
    )(proj, proj, lp["cw"], lp["cb"], lp["wa"], lp["ba"], lp["wx"], lp["bx"], lp["lam"], cbuf, h0)


def _lru_step_kernel(x_ref, g_ref, cw_ref, cb_ref, wa_ref, ba_ref, wx_ref, bx_ref, lam_ref, cbuf_ref, h0_ref,
                     o_ref, hout_ref):
    cw = cw_ref[...]
    xc = cbuf_ref[0] * cw[0:1]
    xc = xc + cbuf_ref[1] * cw[1:2]
    xc = xc + cbuf_ref[2] * cw[2:3]
    xc = xc + x_ref[...] * cw[3:4]
    a, b = _lru_gates(xc + cb_ref[...], wa_ref, ba_ref, wx_ref, bx_ref, lam_ref)
    h = a * h0_ref[...] + b
    hout_ref[...] = h
    o_ref[...] = (h * _gelu(g_ref[...])).astype(o_ref.dtype)


def _lru_step(proj_small, lp, cbuf3, h0):
    rows = h0.shape[0]
    blk = SAMPLE_ROW0 // rows
    vec = pl.BlockSpec((1, LRU_HEAD_DIM), lambda h: (0, h))
    wsp = pl.BlockSpec((1, LRU_HEAD_DIM, LRU_HEAD_DIM), lambda h: (h, 0, 0))
    act = pl.BlockSpec((rows, LRU_HEAD_DIM), lambda h: (0, h))
    return pl.pallas_call(
        _lru_step_kernel,
        grid=(LRU_HEADS,),
        in_specs=[pl.BlockSpec((rows, LRU_HEAD_DIM), lambda h: (blk, XLCOL + h)),
                  pl.BlockSpec((rows, LRU_HEAD_DIM), lambda h: (blk, GLCOL + h)),
                  pl.BlockSpec((CONV_W, LRU_HEAD_DIM), lambda h: (0, h)),
                  vec, wsp, vec, wsp, vec, vec,
                  pl.BlockSpec((CONV_W - 1, rows, LRU_HEAD_DIM), lambda h: (0, 0, h)),
                  act],
        out_specs=[act, act],
        out_shape=[jax.ShapeDtypeStruct((rows, D_LRU), BF16), jax.ShapeDtypeStruct((rows, D_LRU), F32)],
        compiler_params=_cparams(("parallel",)),
        name="lru_step",
    )(proj_small, proj_small, lp["cw"], lp["cb"], lp["wa"], lp["ba"], lp["wx"], lp["bx"], lp["lam"], cbuf3, h0)


def _block_diag_tiles(w):
    g, r, c = w.shape
    wt = w.reshape(N_SSM_TILES, SSM_TILE_GROUPS, r, c)
    eye = jnp.eye(SSM_TILE_GROUPS, dtype=w.dtype)
    return jnp.einsum("tgrc,gh->tgrhc", wt, eye).reshape(N_SSM_TILES, SSM_TILE_GROUPS * r, SSM_TILE_GROUPS * c)


def _s5_params(lam_re, lam_im, log_step, b_re, b_im, c_re, c_im, d):
    dt = jnp.exp(log_step)[:, None]

    def apow(k):
        mag = jnp.exp(k * lam_re * dt)
        ang = k * lam_im * dt
        return (mag * jnp.cos(ang)).reshape(-1), (mag * jnp.sin(ang)).reshape(-1)

    mag = jnp.exp(lam_re * dt)
    ang = lam_im * dt
    ab_re = mag * jnp.cos(ang)
    ab_im = mag * jnp.sin(ang)
    den = lam_re * lam_re + lam_im * lam_im
    f_re = ((ab_re - 1.0) * lam_re + ab_im * lam_im) / den
    f_im = (ab_im * lam_re - (ab_re - 1.0) * lam_im) / den
    bb_re = f_re[..., None] * b_re - f_im[..., None] * b_im
    bb_im = f_re[..., None] * b_im + f_im[..., None] * b_re
    a1 = (ab_re.reshape(-1), ab_im.reshape(-1))
    a2 = apow(2.0)
    a4 = apow(4.0)
    zero = jnp.zeros_like(a1[0])
    ak = jnp.stack([a1[0], a1[1], a2[0], a2[1], a4[0], a4[1], zero, zero])
    chain = [a1] + [apow(float(k)) for k in range(2, 9)]
    apre = jnp.stack([p[0] for p in chain])
    apim = jnp.stack([p[1] for p in chain])

    def tiles(x):
        return x.reshape(x.shape[0], N_SSM_TILES, SSM_TILE_STATE).transpose(1, 0, 2)

    def slab(k, x):
        return jnp.where(jnp.arange(8)[:, None] >= k, x[None, :], 0.0)

    akm = jnp.concatenate([slab(k, part) for k, a in ((1, a1), (2, a2), (4, a4)) for part in a])
    return dict(
        akm=tiles(akm),
        bre=_block_diag_tiles(bb_re.transpose(0, 2, 1)).astype(BF16),
        bim=_block_diag_tiles(bb_im.transpose(0, 2, 1)).astype(BF16),
        cre=_block_diag_tiles(c_re.transpose(0, 2, 1)).astype(BF16),
        cim=_block_diag_tiles(c_im.transpose(0, 2, 1)).astype(BF16),
        d=d.reshape(1, D_SSM), ak=tiles(ak), apre=tiles(apre), apim=tiles(apim))


def _project_in(xn, w_in, layer, *, tm, tn, tk):
    if tk is None:
        q = _matmul_full_k(xn, [w_in], out_dtype=BF16, tm=tm, tn=tn, name="proj_q", layer=layer, col0=0, n=D_ATTN)
        proj = _matmul_full_k(xn, [w_in], out_dtype=F32, tm=tm, tn=tn, name="proj_rest", layer=layer, col0=D_ATTN,
                              n=N_PROJ)
        return q, proj
    q = _matmul(xn, [w_in], out_dtype=BF16, tm=tm, tn=tn, tk=tk, name="proj_q", layer=layer, col0=0, n=D_ATTN)
    proj = _matmul(xn, [w_in], out_dtype=F32, tm=tm, tn=tn, tk=tk, name="proj_rest", layer=layer, col0=D_ATTN,
                   n=N_PROJ)
    return q, proj


def _mix_and_ffn(x, proj, o_attn, zg, zgb, o_lru, wl, layer, g_next, *, tm, tn, tk, rows, full_k=False):
    o_ssm = _matmul(
        zgb, [wl["w_glu"]], out_dtype=BF16, tm=tm, tn=tn, tk=512, epilogue=_glu_epilogue,
        extras=(zg, wl["b_glu"]),
        extra_specs=(pl.BlockSpec((tm, tn), lambda i, j, k: (i, j)), pl.BlockSpec((1, tn), lambda i, j, k: (0, j))),
        name="ssm_glu", layer=layer)
    merged = _merge((o_attn, o_ssm, o_lru), (wl["w_attn_out"], wl["w_ssm_out"], wl["w_lru_out"]), proj,
                    layer=layer, tm=min(tm, 1024), tn=512, tk=tk)
    if full_k:
        mix = _matmul_full_k(merged, [wl["w_out"]], out_dtype=F32, tm=1024, tn=512, name="w_out", layer=layer)
    else:
        mix = _matmul(merged, [wl["w_out"]], out_dtype=F32, tm=tm, tn=tn, tk=tk, name="w_out", layer=layer)
    x, hn = _resid_norm(x, mix, wl["norm_post_mix"], wl["norm_pre_ffn"], rows)
    if full_k:
        hid = _matmul_full_k(hn, [wl["w_ffn_gate"], wl["w_ffn_up"]], out_dtype=BF16, tm=1024, tn=256,
                             name="ffn_in", layer=layer)
    else:
        hid = _matmul(hn, [wl["w_ffn_gate"], wl["w_ffn_up"]], out_dtype=BF16, tm=tm, tn=tn, tk=tk,
                      epilogue=_swiglu_epilogue, name="ffn_in", layer=layer)
    ff = _matmul(hid, [wl["w_ffn_down"]], out_dtype=F32, tm=tm, tn=tn, tk=tk, name="ffn_out", layer=layer)
    return _resid_norm(x, ff, wl["norm_post_ffn"], g_next, rows)


def kernel(x_prompt, x_sample, cache_k, cache_v, page_table, state_ssm_re, state_ssm_im, state_lru_h, state_conv, meta_tokens, w_in, attn_bias, ssm_a_re, ssm_a_im, ssm_log_step, ssm_b_re, ssm_b_im, ssm_c_re, ssm_c_im, ssm_d, ssm_w_glu, ssm_b_glu, lru_conv_w, lru_conv_b, lru_w_a, lru_b_a, lru_w_x, lru_b_x, lru_lambda, w_attn_out, w_ssm_out, w_lru_out, w_out, w_ffn_gate, w_ffn_up, w_ffn_down, norm_pre_mix, norm_post_mix, norm_pre_ffn, norm_post_ffn):
    depth = w_in.shape[0]
    bp, t_main, _ = x_prompt.shape
    bd = x_sample.shape[0]
    n_pool = cache_k.shape[1]
    pad_rows = SMALL_ROWS - SAMPLE_ROW0 - bd
    samp_rows = SMALL_ROWS - SAMPLE_ROW0

    x_main = x_prompt.reshape(bp * t_main, D_MODEL)
    x_small = jnp.concatenate([meta_tokens, x_sample[:, 0, :], jnp.zeros((pad_rows, D_MODEL), F32)], axis=0)
    cache_k4 = cache_k.reshape(depth, n_pool, PAGE_SIZE * N_HEADS, HEAD_DIM)
    cache_v4 = cache_v.reshape(depth, n_pool, PAGE_SIZE * N_HEADS, HEAD_DIM)

    def pad_samples(x):
        return jnp.pad(x, [(0, samp_rows - bd)] + [(0, 0)] * (x.ndim - 1))

    big = dict(tm=2048, tn=1024, tk=1024, rows=256, full_k=True)
    small = dict(tm=SMALL_ROWS, tn=1024, tk=1024, rows=SMALL_ROWS)

    xn_main = _norm(x_main, norm_pre_mix[0], big["rows"])
    xn_small = _norm(x_small, norm_pre_mix[0], small["rows"])

    outs = {name: [] for name in ("kp", "vp", "ks", "vs", "srp", "sip", "srs", "sis", "lhp", "lhs", "cvp", "cvs")}
    for l in range(depth):
        wl = dict(
            w_glu=ssm_w_glu, b_glu=ssm_b_glu[l].reshape(1, D_SSM),
            w_attn_out=w_attn_out, w_ssm_out=w_ssm_out, w_lru_out=w_lru_out, w_out=w_out,
            w_ffn_gate=w_ffn_gate, w_ffn_up=w_ffn_up, w_ffn_down=w_ffn_down,
            norm_post_mix=norm_post_mix[l], norm_pre_ffn=norm_pre_ffn[l], norm_post_ffn=norm_post_ffn[l])
        g_next = norm_pre_mix[l + 1] if l + 1 < depth else None
        sp = _s5_params(ssm_a_re[l], ssm_a_im[l], ssm_log_step[l], ssm_b_re[l], ssm_b_im[l], ssm_c_re[l],
                        ssm_c_im[l], ssm_d[l])
        lp = dict(cw=lru_conv_w[l], cb=lru_conv_b[l].reshape(1, D_LRU), wa=lru_w_a[l].astype(BF16),
                  ba=lru_b_a[l].reshape(1, D_LRU), wx=lru_w_x[l].astype(BF16), bx=lru_b_x[l].reshape(1, D_LRU),
                  lam=lru_lambda[l].reshape(1, D_LRU))
        bias3 = jnp.broadcast_to(attn_bias[l][:, None, None], (N_HEADS, 1, 2 * LANE))
        bias_col = attn_bias[l].reshape(N_HEADS, 1)

        q_s, proj_s = _project_in(xn_small, w_in, l, tm=small["tm"], tn=small["tn"], tk=small["tk"])
        k_meta = proj_s[:N_META, :D_ATTN]
        v_meta = proj_s[:N_META, D_ATTN:2 * D_ATTN]
        kpad = jnp.pad(k_meta, ((0, LANE - N_META), (0, 0)))
        vpad = jnp.pad(v_meta, ((0, LANE - N_META), (0, 0)))
        o_attn_meta = _attn_meta(q_s, kpad, vpad, bias3)
        q_dec = q_s[SAMPLE_ROW0:SAMPLE_ROW0 + bd].reshape(bd, N_HEADS, HEAD_DIM)
        o_attn_dec = _attn_decode(q_dec, cache_k4, cache_v4, page_table, bias_col, l)
        o_attn_s = jnp.concatenate([o_attn_meta, pad_samples(o_attn_dec.reshape(bd, D_ATTN))], axis=0)

        zero_state = jnp.zeros((1, 1, D_STATE), F32)
        zg_m, zgb_m, hre_m, him_m = _s5_seq(proj_s, sp, zero_state, zero_state, batch=1, t=N_META, rc=N_META)
        zg_d, zgb_d, hre_d, him_d = _s5_step(proj_s, sp, pad_samples(state_ssm_re[l].reshape(bd, D_STATE)),
                                             pad_samples(state_ssm_im[l].reshape(bd, D_STATE)))
        zg_s = jnp.concatenate([zg_m, zg_d], axis=0)
        zgb_s = jnp.concatenate([zgb_m, zgb_d], axis=0)

        o_lru_m, lh_m, cv_m = _lru_seq(proj_s, lp, jnp.zeros((1, 8, D_LRU), F32), jnp.zeros((1, 1, D_LRU), F32),
                                       batch=1, t=N_META, rc=N_META)
        cbuf_d = pad_samples(state_conv[l]).transpose(1, 0, 2)
        o_lru_d, lh_d = _lru_step(proj_s, lp, cbuf_d, pad_samples(state_lru_h[l]))
        o_lru_s = jnp.concatenate([o_lru_m, o_lru_d], axis=0)
        x_lru_d = proj_s[SAMPLE_ROW0:SAMPLE_ROW0 + bd, 2 * D_ATTN + D_SSM:2 * D_ATTN + D_SSM + D_LRU]

        x_small, xn_small = _mix_and_ffn(x_small, proj_s, o_attn_s, zg_s, zgb_s, o_lru_s, wl, l, g_next, **small)

        q_m, proj_m = _project_in(xn_main, w_in, l, tm=1024, tn=512, tk=None)
        o_attn = _attn_main(q_m, proj_m, kpad, vpad, bias3, batch=bp, t=t_main, tq=256)
        zg, zgb, hre, him = _s5_seq(proj_m, sp, hre_m, him_m, batch=bp, t=t_main, rc=256)
        cbuf_main = jnp.pad(cv_m, ((0, 0), (8 - (CONV_W - 1), 0), (0, 0)))
        o_lru, lh, cv = _lru_seq(proj_m, lp, cbuf_main, lh_m, batch=bp, t=t_main, rc=256)
        x_main, xn_main = _mix_and_ffn(x_main, proj_m, o_attn, zg, zgb, o_lru, wl, l, g_next, **big)

        def with_meta(main_rows, meta_rows):
            main4 = main_rows.reshape(bp, t_main, N_HEADS, HEAD_DIM)
            meta4 = jnp.broadcast_to(meta_rows.reshape(1, N_META, N_HEADS, HEAD_DIM), (bp, N_META, N_HEADS, HEAD_DIM))
            return jnp.concatenate([meta4, main4], axis=1)

        outs["kp"].append(with_meta(proj_m[:, :D_ATTN], k_meta))
        outs["vp"].append(with_meta(proj_m[:, D_ATTN:2 * D_ATTN], v_meta))
        outs["ks"].append(proj_s[SAMPLE_ROW0:SAMPLE_ROW0 + bd, :D_ATTN].reshape(bd, 1, N_HEADS, HEAD_DIM))
        outs["vs"].append(proj_s[SAMPLE_ROW0:SAMPLE_ROW0 + bd, D_ATTN:2 * D_ATTN].reshape(bd, 1, N_HEADS, HEAD_DIM))
        outs["srp"].append(hre.reshape(bp, SSM_GROUPS, SSM_STATE))
        outs["sip"].append(him.reshape(bp, SSM_GROUPS, SSM_STATE))
        outs["srs"].append(hre_d[:bd].reshape(bd, SSM_GROUPS, SSM_STATE))
        outs["sis"].append(him_d[:bd].reshape(bd, SSM_GROUPS, SSM_STATE))
        outs["lhp"].append(lh.reshape(bp, D_LRU))
        outs["lhs"].append(lh_d[:bd])
        outs["cvp"].append(cv)
        outs["cvs"].append(jnp.concatenate([state_conv[l][:, 1:], x_lru_d[:, None, :]], axis=1))

    y_prompt = x_main.reshape(bp, t_main, D_MODEL)
    y_sample = x_small[SAMPLE_ROW0:SAMPLE_ROW0 + bd].reshape(bd, 1, D_MODEL)
    st = {k: jnp.stack(v) for k, v in outs.items()}
    return (y_prompt, y_sample, st["kp"], st["vp"], st["ks"], st["vs"], st["srp"], st["sip"], st["srs"], st["sis"],
            st["lhp"], st["lhs"], st["cvp"], st["cvs"])
```
